```python
import math
import jax, jax.numpy as jnp
from jax import lax
import numpy as np

D_MODEL = 1024
BATCH = 8
SEQ = 4096
DEPTH = 1

MLSTM_HEADS = 4
MLSTM_HEAD_DIM = 256
MLSTM_WIDTH = MLSTM_HEADS * MLSTM_HEAD_DIM
MLSTM_CHUNK = 128
N_GATE_COLS = 4 * MLSTM_HEADS
LRU_WIDTH = D_MODEL
LRU_BLOCKS = 8
LRU_BLOCK_DIM = LRU_WIDTH // LRU_BLOCKS
LRU_C = 8.0
CONV_WIDTH = 4
CONV_PAD = (2, 1)
PEER_HEADS = 8
PEER_N_KEYS = 128
PEER_N_EXPERTS = PEER_N_KEYS * PEER_N_KEYS
PEER_QUERY_DIM = 256
PEER_HALF = PEER_QUERY_DIM // 2
PEER_TOPK = 16
PEER_TOKEN_BLOCK = 128
EPS = 1e-6

IN_SPLIT_SIZES = (MLSTM_WIDTH, MLSTM_WIDTH, N_GATE_COLS, LRU_WIDTH, LRU_WIDTH, 2 * D_MODEL)
D_IN = MLSTM_WIDTH * 2 + N_GATE_COLS + LRU_WIDTH * 2 + 2 * D_MODEL

kernel_name = "hybrid_mlstm_rglru_peer_encoder"


def rms_norm(x, g):
    x32 = x.astype(jnp.float32)
    y = x32 * lax.rsqrt(jnp.mean(x32 * x32, axis=-1, keepdims=True) + EPS)
    return (y * g.astype(jnp.float32)).astype(x.dtype)


def dwconv_centred(x, w, b):
    y = lax.conv_general_dilated(
        x, w[:, None, :].astype(x.dtype), window_strides=(1,), padding=[CONV_PAD],
        dimension_numbers=("NWC", "WIO", "NWC"), feature_group_count=x.shape[-1])
    return y + b.astype(x.dtype)


def mlstm_scan(q, k, v, log_i, log_f):
    B, H, S, d = q.shape
    nc = S // MLSTM_CHUNK

    def to_chunks(t):
        return jnp.moveaxis(t.reshape((B, H, nc, MLSTM_CHUNK) + t.shape[3:]), 2, 0)

    tril = jnp.tril(jnp.ones((MLSTM_CHUNK, MLSTM_CHUNK), dtype=bool))

    def body(carry, xs):
        C, n, m = carry
        qc, kc, vc, ic, fc = xs
        b = jnp.cumsum(fc, axis=-1)
        d_log = jnp.where(tril, b[..., :, None] - b[..., None, :] + ic[..., None, :], -jnp.inf)
        inter = b + m[..., None]
        m_t = jnp.maximum(inter, jnp.max(d_log, axis=-1))
        w_intra = jnp.exp(d_log - m_t[..., None])
        w_inter = jnp.exp(inter - m_t)
        s = jnp.einsum("bhtd,bhsd->bhts", qc, kc) * w_intra
        num = w_inter[..., None] * jnp.einsum("bhtd,bhde->bhte", qc, C) + jnp.einsum("bhts,bhse->bhte", s, vc)
        den = w_inter * jnp.einsum("bhtd,bhd->bht", qc, n) + jnp.sum(s, axis=-1)
        h = num / jnp.maximum(jnp.abs(den), jnp.exp(-m_t))[..., None]
        g = b[..., -1]
        w_log = g[..., None] - b + ic
        m_new = jnp.maximum(g + m, jnp.max(w_log, axis=-1))
        decay = jnp.exp(g + m - m_new)
        w_state = jnp.exp(w_log - m_new[..., None])
        kw = kc * w_state[..., None]
        C_new = decay[..., None, None] * C + jnp.einsum("bhsd,bhse->bhde", kw, vc)
        n_new = decay[..., None] * n + jnp.sum(kw, axis=2)
        return (C_new, n_new, m_new), h

    init = (jnp.zeros((B, H, d, d), jnp.float32), jnp.zeros((B, H, d), jnp.float32),
            jnp.zeros((B, H), jnp.float32))
    _, h = lax.scan(body, init, (to_chunks(q), to_chunks(k), to_chunks(v), to_chunks(log_i), to_chunks(log_f)))
    return jnp.moveaxis(h, 0, 2).reshape(B, H, S, d)


def mlstm_branch(xm, o_pre, gate_pre, conv_w, conv_b, w_q, w_k, w_v, norm_g):
    B, S, _ = xm.shape
    H, d = MLSTM_HEADS, MLSTM_HEAD_DIM
    xc = jax.nn.silu(dwconv_centred(xm, conv_w, conv_b)).reshape(B, S, H, d)
    xv = xm.reshape(B, S, H, d)
    q = jnp.einsum("bshd,hde->bhse", xc, w_q).astype(jnp.float32)
    k = (jnp.einsum("bshd,hde->bhse", xc, w_k) / math.sqrt(d)).astype(jnp.float32)
    v = jnp.einsum("bshd,hde->bhse", xv, w_v).astype(jnp.float32)
    gates = gate_pre.astype(jnp.float32).reshape(B, S, 4, H).transpose(2, 0, 3, 1)
    i_f, f_f, i_b, f_b = gates[0], gates[1], gates[2], gates[3]
    flip = lambda t: jnp.flip(t, axis=2)
    h_fwd = mlstm_scan(q, k, v, i_f, jax.nn.log_sigmoid(f_f))
    h_bwd = flip(mlstm_scan(flip(q), flip(k), flip(v), flip(i_b), flip(jax.nn.log_sigmoid(f_b))))
    h = jnp.transpose(h_fwd + h_bwd, (0, 2, 1, 3))
    h = rms_norm(h, norm_g.reshape(H, d)).reshape(B, S, MLSTM_WIDTH)
    return (jax.nn.sigmoid(o_pre.astype(jnp.float32)) * h.astype(jnp.float32)).astype(xm.dtype)


def rglru_dir(x, w_r, b_r, w_i, b_i, lam):
    B, S, C = x.shape
    xb = x.reshape(B, S, LRU_BLOCKS, LRU_BLOCK_DIM)
    r = jax.nn.sigmoid(jnp.einsum("bsnc,ncd->bsnd", xb, w_r).reshape(B, S, C) + b_r)
    i = jax.nn.sigmoid(jnp.einsum("bsnc,ncd->bsnd", xb, w_i).reshape(B, S, C) + b_i)
    log_a = -LRU_C * r * jax.nn.softplus(-lam)
    a = jnp.exp(log_a)
    u = jnp.sqrt(-jnp.expm1(2.0 * log_a)) * (i * x)

    def combine(left, right):
        a1, b1 = left
        a2, b2 = right
        return a1 * a2, a2 * b1 + b2

    _, h = lax.associative_scan(combine, (a, u), axis=1)
    return h


def lru_branch(xl, gate_pre, conv_w, conv_b, w_r, b_r, w_i, b_i, lam):
    xc = dwconv_centred(xl, conv_w, conv_b).astype(jnp.float32)
    w_r, b_r, w_i, b_i, lam = (t.astype(jnp.float32) for t in (w_r, b_r, w_i, b_i, lam))
    h_fwd = rglru_dir(xc, w_r[0], b_r[0], w_i[0], b_i[0], lam[0])
    h_bwd = jnp.flip(rglru_dir(jnp.flip(xc, axis=1), w_r[1], b_r[1], w_i[1], b_i[1], lam[1]), axis=1)
    y = (h_fwd + h_bwd) * jax.nn.gelu(gate_pre.astype(jnp.float32), approximate=False)
    return y.astype(xl.dtype)


def peer(h, w_q, sub_keys, expert_u, expert_v):
    B, S, D = h.shape
    nb = (B * S) // PEER_TOKEN_BLOCK
    keys32 = sub_keys.astype(jnp.float32)

    def block_fn(hb):
        tb = hb.shape[0]
        q = jnp.einsum("td,de->te", hb, w_q).astype(jnp.float32).reshape(tb, PEER_HEADS, 2, PEER_HALF)
        scores = jnp.einsum("thpc,hpnc->thpn", q, keys32)
        s_top, i_top = lax.top_k(scores, PEER_TOPK)
        cand = (s_top[:, :, 0, :, None] + s_top[:, :, 1, None, :]).reshape(tb, PEER_HEADS, PEER_TOPK * PEER_TOPK)
        cand_idx = (i_top[:, :, 0, :, None] * PEER_N_KEYS + i_top[:, :, 1, None, :]).reshape(tb, PEER_HEADS, PEER_TOPK * PEER_TOPK)
        best, pos = lax.top_k(cand, PEER_TOPK)
        idx = jnp.take_along_axis(cand_idx, pos, axis=-1).reshape(tb, PEER_HEADS * PEER_TOPK)
        gate = jax.nn.softmax(best, axis=-1).reshape(tb, PEER_HEADS * PEER_TOPK)
        u = jnp.take(expert_u, idx, axis=0)
        act = jax.nn.gelu(jnp.einsum("tkd,td->tk", u, hb).astype(jnp.float32), approximate=False)
        vv = jnp.take(expert_v, idx, axis=0)
        return jnp.einsum("tk,tkd->td", (gate * act).astype(hb.dtype), vv)

    out = lax.map(block_fn, h.reshape(nb, PEER_TOKEN_BLOCK, D))
    return out.reshape(B, S, D)


def setup_inputs(seed: int = 0) -> dict:
    key = jax.random.key(seed)
    ks = jax.random.split(key, 32)
    nrm = lambda k, shape, scale: jax.random.normal(k, shape, jnp.float32) * scale
    L = DEPTH
    f_base = jnp.linspace(3.0, 6.0, MLSTM_HEADS)
    z = jnp.zeros((MLSTM_HEADS,), jnp.float32)
    gate_base = jnp.stack([z, f_base, z, f_base])
    b_gates = (gate_base[None] + nrm(ks[3], (L, 4, MLSTM_HEADS), 0.1)).reshape(L, N_GATE_COLS)
    a0 = jax.random.uniform(ks[16], (L, 2, LRU_WIDTH), jnp.float32, 0.9, 0.999) ** (1.0 / LRU_C)
    lru_lambda = jnp.log(a0) - jnp.log1p(-a0)
    return {
        "x": nrm(ks[0], (BATCH, SEQ, D_MODEL), 1.0),
        "norm1_g": 1.0 + nrm(ks[1], (L, D_MODEL), 0.02),
        "w_in": nrm(ks[2], (L, D_MODEL, D_IN), D_MODEL ** -0.5),
        "b_gates": b_gates,
        "mlstm_conv_w": nrm(ks[4], (L, CONV_WIDTH, MLSTM_WIDTH), CONV_WIDTH ** -0.5),
        "mlstm_conv_b": nrm(ks[5], (L, MLSTM_WIDTH), 0.02),
        "mlstm_w_q": nrm(ks[6], (L, MLSTM_HEADS, MLSTM_HEAD_DIM, MLSTM_HEAD_DIM), MLSTM_HEAD_DIM ** -0.5),
        "mlstm_w_k": nrm(ks[7], (L, MLSTM_HEADS, MLSTM_HEAD_DIM, MLSTM_HEAD_DIM), MLSTM_HEAD_DIM ** -0.5),
        "mlstm_w_v": nrm(ks[8], (L, MLSTM_HEADS, MLSTM_HEAD_DIM, MLSTM_HEAD_DIM), MLSTM_HEAD_DIM ** -0.5),
        "mlstm_norm_g": 1.0 + nrm(ks[9], (L, MLSTM_WIDTH), 0.02),
        "lru_conv_w": nrm(ks[10], (L, CONV_WIDTH, LRU_WIDTH), CONV_WIDTH ** -0.5),
        "lru_conv_b": nrm(ks[11], (L, LRU_WIDTH), 0.02),
        "lru_w_r": nrm(ks[12], (L, 2, LRU_BLOCKS, LRU_BLOCK_DIM, LRU_BLOCK_DIM), LRU_BLOCK_DIM ** -0.5),
        "lru_b_r": nrm(ks[13], (L, 2, LRU_WIDTH), 0.02),
        "lru_w_i": nrm(ks[14], (L, 2, LRU_BLOCKS, LRU_BLOCK_DIM, LRU_BLOCK_DIM), LRU_BLOCK_DIM ** -0.5),
        "lru_b_i": nrm(ks[15], (L, 2, LRU_WIDTH), 0.02),
        "lru_lambda": lru_lambda,
        "w_branch_mlstm": nrm(ks[17], (L, MLSTM_WIDTH, D_MODEL), MLSTM_WIDTH ** -0.5),
        "w_branch_lru": nrm(ks[18], (L, LRU_WIDTH, D_MODEL), LRU_WIDTH ** -0.5),
        "w_out": nrm(ks[19], (L, D_MODEL, D_MODEL), D_MODEL ** -0.5),
        "norm2_g": 1.0 + nrm(ks[20], (L, D_MODEL), 0.02),
        "peer_w_q": nrm(ks[21], (L, D_MODEL, PEER_HEADS * PEER_QUERY_DIM), D_MODEL ** -0.5),
        "peer_sub_keys": nrm(ks[22], (L, PEER_HEADS, 2, PEER_N_KEYS, PEER_HALF), PEER_HALF ** -0.5),
        "peer_u": nrm(ks[23], (L, PEER_N_EXPERTS, D_MODEL), D_MODEL ** -0.5),
        "peer_v": nrm(ks[24], (L, PEER_N_EXPERTS, D_MODEL), PEER_HEADS ** -0.5),
        "final_norm_g": 1.0 + nrm(ks[25], (D_MODEL,), 0.02),
    }


def reference(x, norm1_g, w_in, b_gates, mlstm_conv_w, mlstm_conv_b, mlstm_w_q, mlstm_w_k, mlstm_w_v,
              mlstm_norm_g, lru_conv_w, lru_conv_b, lru_w_r, lru_b_r, lru_w_i, lru_b_i, lru_lambda,
              w_branch_mlstm, w_branch_lru, w_out, norm2_g, peer_w_q, peer_sub_keys, peer_u, peer_v,
              final_norm_g):
    split_points = np.cumsum(IN_SPLIT_SIZES)[:-1].tolist()
    for l in range(DEPTH):
        h = rms_norm(x, norm1_g[l])
        proj = jnp.einsum("bsd,de->bse", h, w_in[l])
        xm, o_pre, gate_pre, xl, lru_gate, merge_pre = jnp.split(proj, split_points, axis=-1)
        gate_pre = gate_pre + b_gates[l].astype(gate_pre.dtype)
        y_m = mlstm_branch(xm, o_pre, gate_pre, mlstm_conv_w[l], mlstm_conv_b[l], mlstm_w_q[l],
                           mlstm_w_k[l], mlstm_w_v[l], mlstm_norm_g[l])
        y_l = lru_branch(xl, lru_gate, lru_conv_w[l], lru_conv_b[l], lru_w_r[l], lru_b_r[l],
                         lru_w_i[l], lru_b_i[l], lru_lambda[l])
        g_m, g_l = jnp.split(jax.nn.sigmoid(merge_pre), 2, axis=-1)
        merged = (g_m * jnp.einsum("bsc,cd->bsd", y_m, w_branch_mlstm[l])
                  + g_l * jnp.einsum("bsc,cd->bsd", y_l, w_branch_lru[l]))
        x = x + jnp.einsum("bsd,de->bse", merged, w_out[l])
        h2 = rms_norm(x, norm2_g[l])
        x = x + peer(h2, peer_w_q[l], peer_sub_keys[l], peer_u[l], peer_v[l])
    return rms_norm(x, final_norm_g)
```

```python
import functools
import math

import jax
import jax.numpy as jnp
from jax import lax
from jax.experimental import pallas as pl
from jax.experimental.pallas import tpu as pltpu

F32 = jnp.float32
BF16 = jnp.bfloat16

D_MODEL = 1024
N_HEADS = 4
HEAD_DIM = 256
N_GATES = 16
LRU_BLOCKS = 8
LRU_BLOCK_DIM = 128
LRU_C = 8.0
PEER_HEADS = 8
PEER_KEYS = 128
PEER_HALF = 128
PEER_TOPK = 16
EPS = 1e-6
LANES = 128
SUBLANES = 8
HALO = SUBLANES
NEG_INF = float("-inf")

VMEM_LIMIT = 56 * 1024 * 1024


def _cparams(sem):
    return pltpu.CompilerParams(dimension_semantics=sem, vmem_limit_bytes=VMEM_LIMIT)


def _sigmoid(x):
    return 1.0 / (1.0 + jnp.exp(-x))


def _gelu(x):
    return 0.5 * x * (1.0 + lax.erf(x * (1.0 / math.sqrt(2.0))))


def _log_sigmoid(x):
    return jnp.minimum(x, 0.0) - jnp.log(1.0 + jnp.exp(-jnp.abs(x)))


def _inproj_body(x_ref, g_ref, w_ref, wg_ref, bg_ref, o_ref, gate_ref, h_scr):
    @pl.when(pl.program_id(1) == 0)
    def _():
        x = x_ref[...]
        ms = jnp.mean(x * x, axis=-1, keepdims=True)
        hb = (x * lax.rsqrt(ms + EPS) * g_ref[...]).astype(BF16)
        h_scr[...] = hb
        gate_ref[...] = jnp.dot(hb, wg_ref[...], preferred_element_type=F32) + bg_ref[...]

    o_ref[...] = jnp.dot(h_scr[...], w_ref[...], preferred_element_type=F32)


def _in_proj(x2d, g1, w_main, w_gate, b_gate, *, tm):
    T = x2d.shape[0]
    n_col = w_main.shape[1] // D_MODEL
    return pl.pallas_call(
        _inproj_body,
        grid=(T // tm, n_col),
        in_specs=[
            pl.BlockSpec((tm, D_MODEL), lambda i, j: (i, 0)),
            pl.BlockSpec((1, D_MODEL), lambda i, j: (0, 0)),
            pl.BlockSpec((D_MODEL, D_MODEL), lambda i, j: (0, j)),
            pl.BlockSpec((D_MODEL, LANES), lambda i, j: (0, 0)),
            pl.BlockSpec((1, LANES), lambda i, j: (0, 0)),
        ],
        out_specs=[
            pl.BlockSpec((tm, D_MODEL), lambda i, j: (i, j)),
            pl.BlockSpec((tm, LANES), lambda i, j: (i, 0)),
        ],
        out_shape=[
            jax.ShapeDtypeStruct((T, n_col * D_MODEL), F32),
            jax.ShapeDtypeStruct((T, LANES), F32),
        ],
        scratch_shapes=[pltpu.VMEM((tm, D_MODEL), BF16)],
        compiler_params=_cparams(("parallel", "arbitrary")),
        name="in_proj",
    )(x2d, g1, w_main, w_gate, b_gate)


def _conv_tile(x, prev, nxt, is_first, is_last, cw_ref, cb_ref, ext_scr, tq):
    ext_scr[0:HALO, :] = jnp.where(is_first, 0.0, prev)
    ext_scr[HALO:HALO + tq, :] = x
    ext_scr[HALO + tq:2 * HALO + tq, :] = jnp.where(is_last, 0.0, nxt)
    acc = cb_ref[...] + cw_ref[0:1, :] * ext_scr[HALO - 2:HALO - 2 + tq, :]
    for j in range(1, 4):
        acc = acc + cw_ref[j:j + 1, :] * ext_scr[HALO - 2 + j:HALO - 2 + j + tq, :]
    return acc


def _halo_specs(col, tq, n_tiles):
    r = tq // HALO

    def specs(tile_of):
        return [
            pl.BlockSpec((tq, D_MODEL), lambda *a: (tile_of(*a), col)),
            pl.BlockSpec((HALO, D_MODEL), lambda *a: (jnp.maximum(tile_of(*a) * r - 1, 0), col)),
            pl.BlockSpec((HALO, D_MODEL),
                         lambda *a: (jnp.minimum((tile_of(*a) + 1) * r, n_tiles * r - 1), col)),
        ]

    return specs


def _qkv_body(xm_ref, prev_ref, next_ref, cw_ref, cb_ref, wq_ref, wk_ref, wv_ref,
              q_ref, kt_ref, v_ref, ext_scr, *, tq, tiles_per_seq):
    i = pl.program_id(0)
    pos = i % tiles_per_seq
    xm = xm_ref[...]
    acc = _conv_tile(xm, prev_ref[...], next_ref[...], pos == 0, pos == tiles_per_seq - 1,
                     cw_ref, cb_ref, ext_scr, tq)
    xc = acc * _sigmoid(acc)
    for h in range(N_HEADS):
        sl = slice(h * HEAD_DIM, (h + 1) * HEAD_DIM)
        xch = xc[:, sl].astype(BF16)
        q_ref[:, sl] = jnp.dot(xch, wq_ref[h], preferred_element_type=F32).astype(BF16)
        kh = jnp.dot(xch, wk_ref[h], preferred_element_type=F32) * (1.0 / math.sqrt(HEAD_DIM))
        kt_ref[sl, :] = kh.T.astype(BF16)
        v_ref[:, sl] = jnp.dot(xm[:, sl].astype(BF16), wv_ref[h],
                               preferred_element_type=F32).astype(BF16)


def _qkv(proj, cw, cb, wq, wk, wv, *, S, tq):
    T = proj.shape[0]
    n_tiles = T // tq
    main, prev, nxt = _halo_specs(0, tq, n_tiles)(lambda i: i)
    wspec = pl.BlockSpec((N_HEADS, HEAD_DIM, HEAD_DIM), lambda i: (0, 0, 0))
    return pl.pallas_call(
        functools.partial(_qkv_body, tq=tq, tiles_per_seq=S // tq),
        grid=(n_tiles,),
        in_specs=[main, prev, nxt,
                  pl.BlockSpec((4, D_MODEL), lambda i: (0, 0)),
                  pl.BlockSpec((1, D_MODEL), lambda i: (0, 0)),
                  wspec, wspec, wspec],
        out_specs=[
            pl.BlockSpec((tq, D_MODEL), lambda i: (i, 0)),
            pl.BlockSpec((D_MODEL, tq), lambda i: (0, i)),
            pl.BlockSpec((tq, D_MODEL), lambda i: (i, 0)),
        ],
        out_shape=[
            jax.ShapeDtypeStruct((T, D_MODEL), BF16),
            jax.ShapeDtypeStruct((D_MODEL, T), BF16),
            jax.ShapeDtypeStruct((T, D_MODEL), BF16),
        ],
        scratch_shapes=[pltpu.VMEM((tq + 2 * HALO, D_MODEL), F32)],
        compiler_params=_cparams(("parallel",)),
        name="mlstm_qkv",
    )(proj, proj, proj, cw, cb, wq, wk, wv)


def _mlstm_dir(q_ref, kt_ref, v_ref, g_ref, h_ref, c_scr, n_scr, m_scr, *, L, rev):
    G = g_ref[...]
    logf = _log_sigmoid(G)
    row = lax.broadcasted_iota(jnp.int32, (L, L), 0)
    col = lax.broadcasted_iota(jnp.int32, (L, L), 1)
    mask = (row <= col) if rev else (row >= col)
    cum = jnp.dot(mask.astype(F32), logf, precision=lax.Precision.HIGHEST,
                  preferred_element_type=F32)
    GT = G.T
    cumT = cum.T
    for h in range(N_HEADS):
        hd = h + (N_HEADS if rev else 0)
        icol = (2 * N_HEADS if rev else 0) + h
        fcol = icol + N_HEADS
        sl = slice(h * HEAD_DIM, (h + 1) * HEAD_DIM)
        b_col = cum[:, fcol:fcol + 1]
        r_row = GT[icol:icol + 1, :] - cumT[fcol:fcol + 1, :]
        m = m_scr[hd][:, 0:1]
        d_log = jnp.where(mask, b_col + r_row, NEG_INF)
        inter = b_col + m
        m_t = jnp.maximum(inter, jnp.max(d_log, axis=-1, keepdims=True))
        w_intra = jnp.exp(d_log - m_t)
        w_inter = jnp.exp(inter - m_t)
        qc = q_ref[:, sl]
        kt = kt_ref[sl, :]
        vc = v_ref[:, sl]
        s = jnp.dot(qc, kt, preferred_element_type=F32) * w_intra
        num = (w_inter * jnp.dot(qc, c_scr[hd].astype(BF16), preferred_element_type=F32)
               + jnp.dot(s.astype(BF16), vc, preferred_element_type=F32))
        qn = jnp.dot(qc, n_scr[hd].astype(BF16), preferred_element_type=F32)
        den = w_inter * qn[:, 0:1] + jnp.sum(s, axis=-1, keepdims=True)
        h_ref[:, sl] = num / jnp.maximum(jnp.abs(den), jnp.exp(-m_t))
        g = b_col[0:1, :] if rev else b_col[L - 1:L, :]
        w_row = g + r_row
        m_new = jnp.maximum(g + m, jnp.max(w_row, axis=-1, keepdims=True))
        decay = jnp.exp(g + m - m_new)
        kwt = kt.astype(F32) * jnp.exp(w_row - m_new)
        c_scr[hd] = decay * c_scr[hd] + jnp.dot(kwt.astype(BF16), vc, preferred_element_type=F32)
        n_scr[hd] = decay * n_scr[hd] + jnp.sum(kwt, axis=-1, keepdims=True)
        m_scr[hd] = jnp.broadcast_to(m_new, (1, LANES))


def _mlstm_body(qf, ktf, vf, gf, qb, ktb, vb, gb, hf_ref, hb_ref, c_scr, n_scr, m_scr, *, L):
    @pl.when(pl.program_id(1) == 0)
    def _():
        c_scr[...] = jnp.zeros_like(c_scr)
        n_scr[...] = jnp.zeros_like(n_scr)
        m_scr[...] = jnp.zeros_like(m_scr)

    _mlstm_dir(qf, ktf, vf, gf, hf_ref, c_scr, n_scr, m_scr, L=L, rev=False)
    _mlstm_dir(qb, ktb, vb, gb, hb_ref, c_scr, n_scr, m_scr, L=L, rev=True)


def _mlstm_scan(q, kt, v, gates, *, B, S, L):
    T = B * S
    nc = S // L
    fwd = lambda b, c: b * nc + c
    bwd = lambda b, c: b * nc + (nc - 1 - c)

    def specs(idx):
        return [
            pl.BlockSpec((L, D_MODEL), lambda b, c: (idx(b, c), 0)),
            pl.BlockSpec((D_MODEL, L), lambda b, c: (0, idx(b, c))),
            pl.BlockSpec((L, D_MODEL), lambda b, c: (idx(b, c), 0)),
            pl.BlockSpec((L, LANES), lambda b, c: (idx(b, c), 0)),
        ]

    return pl.pallas_call(
        functools.partial(_mlstm_body, L=L),
        grid=(B, nc),
        in_specs=specs(fwd) + specs(bwd),
        out_specs=[
            pl.BlockSpec((L, D_MODEL), lambda b, c: (fwd(b, c), 0)),
            pl.BlockSpec((L, D_MODEL), lambda b, c: (bwd(b, c), 0)),
        ],
        out_shape=[jax.ShapeDtypeStruct((T, D_MODEL), F32)] * 2,
        scratch_shapes=[
            pltpu.VMEM((2 * N_HEADS, HEAD_DIM, HEAD_DIM), F32),
            pltpu.VMEM((2 * N_HEADS, HEAD_DIM, LANES), F32),
            pltpu.VMEM((2 * N_HEADS, 1, LANES), F32),
        ],
        compiler_params=_cparams(("parallel", "arbitrary")),
        name="mlstm_scan",
    )(q, kt, v, gates, q, kt, v, gates)


def _lru_dir(x, prev, nxt, is_first, is_last, cw_ref, cb_ref, wr_ref, br_ref, wi_ref, bi_ref,
             lam_ref, h_ref, ext_scr, a_scr, u_scr, carry_scr, *, ts, d, rev):
    xc = _conv_tile(x, prev, nxt, is_first, is_last, cw_ref, cb_ref, ext_scr, ts)
    xcb = xc.astype(BF16)
    lam = lam_ref[d:d + 1, :]
    sp = jnp.maximum(-lam, 0.0) + jnp.log(1.0 + jnp.exp(-jnp.abs(lam)))
    for n in range(LRU_BLOCKS):
        sl = slice(n * LRU_BLOCK_DIM, (n + 1) * LRU_BLOCK_DIM)
        xb = xcb[:, sl]
        r = _sigmoid(jnp.dot(xb, wr_ref[d, n], preferred_element_type=F32) + br_ref[d:d + 1, sl])
        ig = _sigmoid(jnp.dot(xb, wi_ref[d, n], preferred_element_type=F32) + bi_ref[d:d + 1, sl])
        a = jnp.exp(-LRU_C * r * sp[:, sl])
        a_scr[:, sl] = a
        u_scr[:, sl] = jnp.sqrt(1.0 - a * a) * (ig * xc[:, sl])

    sub = lax.broadcasted_iota(jnp.int32, (ts, D_MODEL), 0) % SUBLANES
    A = a_scr[...]
    U = u_scr[...]
    for step in (1, 2, 4):
        if rev:
            a_sh = pltpu.roll(A, ts - step, axis=0)
            u_sh = pltpu.roll(U, ts - step, axis=0)
            ok = sub < SUBLANES - step
        else:
            a_sh = pltpu.roll(A, step, axis=0)
            u_sh = pltpu.roll(U, step, axis=0)
            ok = sub >= step
        U = jnp.where(ok, A * u_sh + U, U)
        A = jnp.where(ok, A * a_sh, A)
    a_scr[...] = A
    u_scr[...] = U

    n_grp = ts // SUBLANES

    def body(k, carry):
        g = (n_grp - 1 - k) if rev else k
        r0 = pl.multiple_of(g * SUBLANES, SUBLANES)
        hg = a_scr[pl.ds(r0, SUBLANES), :] * carry + u_scr[pl.ds(r0, SUBLANES), :]
        h_ref[pl.ds(r0, SUBLANES), :] = hg
        return hg[0:1, :] if rev else hg[SUBLANES - 1:SUBLANES, :]

    carry_scr[d:d + 1, :] = lax.fori_loop(0, n_grp, body, carry_scr[d:d + 1, :])


def _lru_body(xf, pf, nf, xb, pb, nb, cw_ref, cb_ref, wr_ref, br_ref, wi_ref, bi_ref, lam_ref,
              hf_ref, hb_ref, ext_scr, a_scr, u_scr, carry_scr, *, ts, nt):
    i = pl.program_id(1)

    @pl.when(i == 0)
    def _():
        carry_scr[...] = jnp.zeros_like(carry_scr)

    common = (cw_ref, cb_ref, wr_ref, br_ref, wi_ref, bi_ref, lam_ref)
    _lru_dir(xf[...], pf[...], nf[...], i == 0, i == nt - 1, *common, hf_ref,
             ext_scr, a_scr, u_scr, carry_scr, ts=ts, d=0, rev=False)
    _lru_dir(xb[...], pb[...], nb[...], i == nt - 1, i == 0, *common, hb_ref,
             ext_scr, a_scr, u_scr, carry_scr, ts=ts, d=1, rev=True)


def _lru(proj, cw, cb, wr, br, wi, bi, lam, *, B, S, ts):
    T = B * S
    nt = S // ts
    fwd = lambda b, i: b * nt + i
    bwd = lambda b, i: b * nt + (nt - 1 - i)
    mk = _halo_specs(2, ts, B * nt)
    full = lambda shape: pl.BlockSpec(shape, lambda b, i: (0,) * len(shape))
    return pl.pallas_call(
        functools.partial(_lru_body, ts=ts, nt=nt),
        grid=(B, nt),
        in_specs=mk(fwd) + mk(bwd) + [
            full((4, D_MODEL)), full((1, D_MODEL)),
            full((2, LRU_BLOCKS, LRU_BLOCK_DIM, LRU_BLOCK_DIM)), full((2, D_MODEL)),
            full((2, LRU_BLOCKS, LRU_BLOCK_DIM, LRU_BLOCK_DIM)), full((2, D_MODEL)),
            full((2, D_MODEL)),
        ],
        out_specs=[
            pl.BlockSpec((ts, D_MODEL), lambda b, i: (fwd(b, i), 0)),
            pl.BlockSpec((ts, D_MODEL), lambda b, i: (bwd(b, i), 0)),
        ],
        out_shape=[jax.ShapeDtypeStruct((T, D_MODEL), F32)] * 2,
        scratch_shapes=[
            pltpu.VMEM((ts + 2 * HALO, D_MODEL), F32),
            pltpu.VMEM((ts, D_MODEL), F32),
            pltpu.VMEM((ts, D_MODEL), F32),
            pltpu.VMEM((2, D_MODEL), F32),
        ],
        compiler_params=_cparams(("parallel", "arbitrary")),
        name="rglru",
    )(proj, proj, proj, proj, proj, proj, cw, cb, wr, br, wi, bi, lam)


def _merge_body(x_ref, hmf_ref, hmb_ref, opre_ref, hlf_ref, hlb_ref, lgate_ref, gm_ref, gl_ref,
                ng_ref, wbm_ref, wbl_ref, wout_ref, g2_ref, x1_ref, h2t_ref):
    hm = hmf_ref[...] + hmb_ref[...]
    parts = []
    for h in range(N_HEADS):
        sl = slice(h * HEAD_DIM, (h + 1) * HEAD_DIM)
        hh = hm[:, sl]
        ms = jnp.mean(hh * hh, axis=-1, keepdims=True)
        parts.append(hh * lax.rsqrt(ms + EPS) * ng_ref[:, sl])
    hn = jnp.concatenate(parts, axis=-1)
    y_m = (_sigmoid(opre_ref[...]) * hn).astype(BF16)
    y_l = ((hlf_ref[...] + hlb_ref[...]) * _gelu(lgate_ref[...])).astype(BF16)
    merged = (_sigmoid(gm_ref[...]) * jnp.dot(y_m, wbm_ref[...], preferred_element_type=F32)
              + _sigmoid(gl_ref[...]) * jnp.dot(y_l, wbl_ref[...], preferred_element_type=F32))
    x1 = x_ref[...] + jnp.dot(merged.astype(BF16), wout_ref[...], preferred_element_type=F32)
    x1_ref[...] = x1
    ms = jnp.mean(x1 * x1, axis=-1, keepdims=True)
    h2 = x1 * lax.rsqrt(ms + EPS) * g2_ref[...]
    h2t_ref[...] = h2.T.astype(BF16)


def _merge(x2d, hmf, hmb, proj, hlf, hlb, ng, wbm, wbl, wout, g2, *, tm):
    T = x2d.shape[0]
    tile = lambda col: pl.BlockSpec((tm, D_MODEL), lambda i: (i, col))
    vec = pl.BlockSpec((1, D_MODEL), lambda i: (0, 0))
    mat = pl.BlockSpec((D_MODEL, D_MODEL), lambda i: (0, 0))
    return pl.pallas_call(
        _merge_body,
        grid=(T // tm,),
        in_specs=[tile(0), tile(0), tile(0), tile(1), tile(0), tile(0), tile(3), tile(4), tile(5),
                  vec, mat, mat, mat, vec],
        out_specs=[
            pl.BlockSpec((tm, D_MODEL), lambda i: (i, 0)),
            pl.BlockSpec((D_MODEL, tm), lambda i: (0, i)),
        ],
        out_shape=[
            jax.ShapeDtypeStruct((T, D_MODEL), F32),
            jax.ShapeDtypeStruct((D_MODEL, T), BF16),
        ],
        compiler_params=_cparams(("parallel",)),
        name="merge",
    )(x2d, hmf, hmb, proj, hlf, hlb, proj, proj, proj, ng, wbm, wbl, wout, g2)


def _top16_ranked(s):
    n, tb = s.shape
    iota = lax.broadcasted_iota(jnp.int32, (n, tb), 0).astype(F32)
    rank = jnp.full((n, tb), float(PEER_TOPK), F32)
    vals = []
    for j in range(PEER_TOPK):
        m = jnp.max(s, axis=0, keepdims=True)
        idx = jnp.min(jnp.where(s == m, iota, float(n)), axis=0, keepdims=True)
        sel = iota == idx
        rank = jnp.where(sel, float(j), rank)
        s = jnp.where(sel, NEG_INF, s)
        vals.append(m)
    return vals, rank


_CAND_ROW_GROUPS = [(0, 0, 8), (0, 8, 8), (1, 0, 8), (2, 0, 5), (3, 0, 4), (4, 0, 3),
                    (5, 0, 2), (6, 0, 2), (7, 0, 2)]


def _select_pairs(v1, v2):
    tb = v1[0].shape[1]
    v1_lo = jnp.concatenate(v1[:8], axis=0)
    v1_hi = jnp.concatenate(v1[8:], axis=0)
    v2_lo = jnp.concatenate(v2[:8], axis=0)
    v2_hi = jnp.concatenate(v2[8:], axis=0)
    sub = lax.broadcasted_iota(jnp.int32, (SUBLANES, tb), 0)
    cands, poss = [], []
    for j1, base, nv in _CAND_ROW_GROUPS:
        c = v1[j1] + (v2_lo if base == 0 else v2_hi)
        cands.append(jnp.where(sub < nv, c, NEG_INF))
        poss.append((sub + (j1 * PEER_TOPK + base)).astype(F32))
    cands.append(v1_hi + v2[0])
    poss.append(((sub + SUBLANES) * PEER_TOPK).astype(F32))
    cand = jnp.concatenate(cands, axis=0)
    pos = jnp.concatenate(poss, axis=0)
    cmax = v1[0] + v2[0]
    p_exp = jnp.exp(cand - cmax)
    taken = jnp.zeros_like(cand)
    big = float(PEER_TOPK * PEER_TOPK)
    for _ in range(PEER_TOPK):
        m = jnp.max(cand, axis=0, keepdims=True)
        p = jnp.min(jnp.where(cand == m, pos, big), axis=0, keepdims=True)
        sel = pos == p
        taken = jnp.where(sel, 1.0, taken)
        cand = jnp.where(sel, NEG_INF, cand)
    Z = jnp.sum(taken * p_exp, axis=0, keepdims=True)
    grp = lambda g: taken[g * SUBLANES:(g + 1) * SUBLANES, :]
    L = [jnp.sum(grp(0) + grp(1), axis=0, keepdims=True)]
    for g in range(2, 9):
        L.append(jnp.sum(grp(g), axis=0, keepdims=True))
    last = grp(9)
    for k in range(SUBLANES):
        L.append(last[k:k + 1, :])
    return L, Z


def _peer_body(h2t_ref, x1_ref, wqt_ref, keys_ref, u_ref, vt_ref, gf_ref, y_ref,
               r2_scr, e2_scr, lim_scr, coef_scr, wa_scr, acc_scr, *, rows_per_chunk):
    j = pl.program_id(1)
    tb = h2t_ref.shape[1]

    @pl.when(j == 0)
    def _():
        h2t = h2t_ref[...]
        for h in range(PEER_HEADS):
            qt = jnp.dot(wqt_ref[h * 2 * PEER_HALF:(h + 1) * 2 * PEER_HALF, :], h2t,
                         preferred_element_type=F32).astype(BF16)
            s1 = jnp.dot(keys_ref[2 * h], qt[:PEER_HALF], preferred_element_type=F32)
            s2 = jnp.dot(keys_ref[2 * h + 1], qt[PEER_HALF:], preferred_element_type=F32)
            v1, rank1 = _top16_ranked(s1)
            v2, rank2 = _top16_ranked(s2)
            L, Z = _select_pairs(v1, v2)
            lim = jnp.zeros_like(rank1)
            for j1 in range(PEER_TOPK):
                lim = jnp.where(rank1 == float(j1), L[j1], lim)
            r2_scr[h] = rank2
            e2_scr[h] = jnp.exp(s2 - v2[0])
            lim_scr[h] = lim
            coef_scr[h] = jnp.exp(s1 - v1[0]) / Z
        acc_scr[...] = jnp.zeros_like(acc_scr)

    act = _gelu(jnp.dot(u_ref[...], h2t_ref[...], preferred_element_type=F32))
    for ii in range(rows_per_chunk):
        i1 = j * rows_per_chunk + ii
        w = jnp.zeros((PEER_KEYS, tb), F32)
        for h in range(PEER_HEADS):
            lim = lim_scr[h, pl.ds(i1, 1), :]
            coef = coef_scr[h, pl.ds(i1, 1), :]
            w = w + jnp.where(r2_scr[h] < lim, e2_scr[h] * coef, 0.0)
        sl = slice(ii * PEER_KEYS, (ii + 1) * PEER_KEYS)
        wa_scr[sl, :] = (w * act[sl, :]).astype(BF16)
    acc_scr[...] += jnp.dot(vt_ref[...], wa_scr[...], preferred_element_type=F32)

    @pl.when(j == pl.num_programs(1) - 1)
    def _():
        x2 = x1_ref[...] + acc_scr[...].T
        ms = jnp.mean(x2 * x2, axis=-1, keepdims=True)
        y_ref[...] = x2 * lax.rsqrt(ms + EPS) * gf_ref[...]


def _peer(h2t, x1, wqt, keys, u, vt, gf, *, tb, ec):
    T = x1.shape[0]
    n_exp = u.shape[0]
    n_q = wqt.shape[0]
    sel = lambda: pltpu.VMEM((PEER_HEADS, PEER_KEYS, tb), F32)
    return pl.pallas_call(
        functools.partial(_peer_body, rows_per_chunk=ec // PEER_KEYS),
        grid=(T // tb, n_exp // ec),
        in_specs=[
            pl.BlockSpec((D_MODEL, tb), lambda i, j: (0, i)),
            pl.BlockSpec((tb, D_MODEL), lambda i, j: (i, 0)),
            pl.BlockSpec((n_q, D_MODEL), lambda i, j: (0, 0)),
            pl.BlockSpec((2 * PEER_HEADS, PEER_KEYS, PEER_HALF), lambda i, j: (0, 0, 0)),
            pl.BlockSpec((ec, D_MODEL), lambda i, j: (j, 0)),
            pl.BlockSpec((D_MODEL, ec), lambda i, j: (0, j)),
            pl.BlockSpec((1, D_MODEL), lambda i, j: (0, 0)),
        ],
        out_specs=pl.BlockSpec((tb, D_MODEL), lambda i, j: (i, 0)),
        out_shape=jax.ShapeDtypeStruct((T, D_MODEL), F32),
        scratch_shapes=[sel(), sel(), sel(), sel(),
                        pltpu.VMEM((ec, tb), BF16),
                        pltpu.VMEM((D_MODEL, tb), F32)],
        compiler_params=_cparams(("parallel", "arbitrary")),
        name="peer",
    )(h2t, x1, wqt, keys, u, vt, gf)


def _layer(x, p, *, tm_in, tq, L, ts, tm_merge, tb, ec):
    B, S, D = x.shape
    T = B * S
    x2d = x.reshape(T, D)
    w_in = p["w_in"]
    g0 = 2 * D_MODEL
    w_main = jnp.concatenate([w_in[:, :g0], w_in[:, g0 + N_GATES:]], axis=1).astype(BF16)
    w_gate = jnp.pad(w_in[:, g0:g0 + N_GATES], ((0, 0), (0, LANES - N_GATES))).astype(BF16)
    b_gate = jnp.pad(p["b_gates"], (0, LANES - N_GATES)).reshape(1, LANES)
    row = lambda a: a.reshape(1, -1)

    proj, gates = _in_proj(x2d, row(p["norm1_g"]), w_main, w_gate, b_gate, tm=tm_in)
    q, kt, v = _qkv(proj, p["mlstm_conv_w"], row(p["mlstm_conv_b"]),
                    p["mlstm_w_q"].astype(BF16), p["mlstm_w_k"].astype(BF16),
                    p["mlstm_w_v"].astype(BF16), S=S, tq=tq)
    hmf, hmb = _mlstm_scan(q, kt, v, gates, B=B, S=S, L=L)
    hlf, hlb = _lru(proj, p["lru_conv_w"], row(p["lru_conv_b"]),
                    p["lru_w_r"].astype(BF16), p["lru_b_r"], p["lru_w_i"].astype(BF16),
                    p["lru_b_i"], p["lru_lambda"], B=B, S=S, ts=ts)
    x1, h2t = _merge(x2d, hmf, hmb, proj, hlf, hlb, row(p["mlstm_norm_g"]),
                     p["w_branch_mlstm"].astype(BF16), p["w_branch_lru"].astype(BF16),
                     p["w_out"].astype(BF16), row(p["norm2_g"]), tm=tm_merge)
    n_keys = PEER_HEADS * 2
    keys = p["peer_sub_keys"].reshape(n_keys, PEER_KEYS, PEER_HALF).astype(BF16)
    y = _peer(h2t, x1, p["peer_w_q"].T.astype(BF16), keys, p["peer_u"].astype(BF16),
              p["peer_v"].T.astype(BF16), row(p["final_norm_g"]), tb=tb, ec=ec)
    return y.reshape(B, S, D)


def kernel(x, norm1_g, w_in, b_gates, mlstm_conv_w, mlstm_conv_b, mlstm_w_q, mlstm_w_k, mlstm_w_v,
           mlstm_norm_g, lru_conv_w, lru_conv_b, lru_w_r, lru_b_r, lru_w_i, lru_b_i, lru_lambda,
           w_branch_mlstm, w_branch_lru, w_out, norm2_g, peer_w_q, peer_sub_keys, peer_u, peer_v,
           final_norm_g):
    p = dict(norm1_g=norm1_g[0], w_in=w_in[0], b_gates=b_gates[0], mlstm_conv_w=mlstm_conv_w[0],
             mlstm_conv_b=mlstm_conv_b[0], mlstm_w_q=mlstm_w_q[0], mlstm_w_k=mlstm_w_k[0],
             mlstm_w_v=mlstm_w_v[0], mlstm_norm_g=mlstm_norm_g[0], lru_conv_w=lru_conv_w[0],
             lru_conv_b=lru_conv_b[0], lru_w_r=lru_w_r[0], lru_b_r=lru_b_r[0], lru_w_i=lru_w_i[0],
             lru_b_i=lru_b_i[0], lru_lambda=lru_lambda[0], w_branch_mlstm=w_branch_mlstm[0],
             w_branch_lru=w_branch_lru[0], w_out=w_out[0], norm2_g=norm2_g[0],
             peer_w_q=peer_w_q[0], peer_sub_keys=peer_sub_keys[0], peer_u=peer_u[0],
             peer_v=peer_v[0], final_norm_g=final_norm_g)
    return _layer(x, p, tm_in=1024, tq=512, L=128, ts=512, tm_merge=256, tb=512, ec=1024)
```

```python
import functools
import math

import jax
import jax.numpy as jnp
from jax import lax
from jax.experimental import pallas as pl
from jax.experimental.pallas import tpu as pltpu

F32 = jnp.float32
BF16 = jnp.bfloat16

D_MODEL = 1024
N_HEADS = 4
HEAD_DIM = 256
N_GATES = 16
LRU_BLOCKS = 8
LRU_BLOCK_DIM = 128
LRU_C = 8.0
PEER_HEADS = 8
PEER_KEYS = 128
PEER_HALF = 128
PEER_TOPK = 16
EPS = 1e-6
LANES = 128
SUBLANES = 8
BF16_ROWS = 16
HALO = SUBLANES
NEG_INF = float("-inf")

VMEM_LIMIT = 56 * 1024 * 1024


def _cparams(sem, flags=None):
    return pltpu.CompilerParams(dimension_semantics=sem, vmem_limit_bytes=VMEM_LIMIT, flags=flags)


def _sigmoid(x):
    return 0.5 * jnp.tanh(0.5 * x) + 0.5


def _gelu(x):
    return 0.5 * x * (1.0 + lax.erf(x * (1.0 / math.sqrt(2.0))))


def _log_sigmoid(x):
    return jnp.minimum(x, 0.0) - jnp.log(1.0 + jnp.exp(-jnp.abs(x)))


def _inproj_body(x_ref, g_ref, w_ref, wg_ref, bg_ref, o_ref, gate_ref, h_scr):
    @pl.when(pl.program_id(1) == 0)
    def _():
        x = x_ref[...]
        ms = jnp.mean(x * x, axis=-1, keepdims=True)
        hb = (x * lax.rsqrt(ms + EPS) * g_ref[...]).astype(BF16)
        h_scr[...] = hb
        gate_ref[...] = jnp.dot(hb, wg_ref[...], preferred_element_type=F32) + bg_ref[...]

    o_ref[...] = jnp.dot(h_scr[...], w_ref[...], preferred_element_type=F32)


def _in_proj(x2d, g1, w_main, w_gate, b_gate, *, tm):
    T = x2d.shape[0]
    n_col = w_main.shape[1] // D_MODEL
    return pl.pallas_call(
        _inproj_body,
        grid=(T // tm, n_col),
        in_specs=[
            pl.BlockSpec((tm, D_MODEL), lambda i, j: (i, 0)),
            pl.BlockSpec((1, D_MODEL), lambda i, j: (0, 0)),
            pl.BlockSpec((D_MODEL, D_MODEL), lambda i, j: (0, j)),
            pl.BlockSpec((D_MODEL, LANES), lambda i, j: (0, 0)),
            pl.BlockSpec((1, LANES), lambda i, j: (0, 0)),
        ],
        out_specs=[
            pl.BlockSpec((tm, D_MODEL), lambda i, j: (i, j)),
            pl.BlockSpec((tm, LANES), lambda i, j: (i, 0)),
        ],
        out_shape=[
            jax.ShapeDtypeStruct((T, n_col * D_MODEL), F32),
            jax.ShapeDtypeStruct((T, LANES), F32),
        ],
        scratch_shapes=[pltpu.VMEM((tm, D_MODEL), BF16)],
        compiler_params=_cparams(("parallel", "arbitrary")),
        name="in_proj",
    )(x2d, g1, w_main, w_gate, b_gate)


def _conv_tile(x, prev, nxt, is_first, is_last, cw_ref, cb_ref, ext_scr, tq):
    ext_scr[0:HALO, :] = jnp.where(is_first, 0.0, prev)
    ext_scr[HALO:HALO + tq, :] = x
    ext_scr[HALO + tq:2 * HALO + tq, :] = jnp.where(is_last, 0.0, nxt)
    acc = cb_ref[...] + cw_ref[0:1, :] * ext_scr[HALO - 2:HALO - 2 + tq, :]
    for j in range(1, 4):
        acc = acc + cw_ref[j:j + 1, :] * ext_scr[HALO - 2 + j:HALO - 2 + j + tq, :]
    return acc


def _halo_specs(col, tq, n_tiles):
    r = tq // HALO

    def specs(tile_of):
        return [
            pl.BlockSpec((tq, D_MODEL), lambda *a: (tile_of(*a), col)),
            pl.BlockSpec((HALO, D_MODEL), lambda *a: (jnp.maximum(tile_of(*a) * r - 1, 0), col)),
            pl.BlockSpec((HALO, D_MODEL),
                         lambda *a: (jnp.minimum((tile_of(*a) + 1) * r, n_tiles * r - 1), col)),
        ]

    return specs


def _qkv_body(xm_ref, prev_ref, next_ref, cw_ref, cb_ref, wq_ref, wk_ref, wv_ref,
              q_ref, kt_ref, v_ref, ext_scr, *, tq, tiles_per_seq):
    i = pl.program_id(0)
    pos = i % tiles_per_seq
    xm = xm_ref[...]
    acc = _conv_tile(xm, prev_ref[...], next_ref[...], pos == 0, pos == tiles_per_seq - 1,
                     cw_ref, cb_ref, ext_scr, tq)
    xc = acc * _sigmoid(acc)
    for h in range(N_HEADS):
        sl = slice(h * HEAD_DIM, (h + 1) * HEAD_DIM)
        xch = xc[:, sl].astype(BF16)
        q_ref[:, sl] = jnp.dot(xch, wq_ref[h], preferred_element_type=F32).astype(BF16)
        kh = jnp.dot(xch, wk_ref[h], preferred_element_type=F32) * (1.0 / math.sqrt(HEAD_DIM))
        kt_ref[sl, :] = kh.T.astype(BF16)
        v_ref[:, sl] = jnp.dot(xm[:, sl].astype(BF16), wv_ref[h],
                               preferred_element_type=F32).astype(BF16)


def _qkv(proj, cw, cb, wq, wk, wv, *, S, tq):
    T = proj.shape[0]
    n_tiles = T // tq
    main, prev, nxt = _halo_specs(0, tq, n_tiles)(lambda i: i)
    wspec = pl.BlockSpec((N_HEADS, HEAD_DIM, HEAD_DIM), lambda i: (0, 0, 0))
    return pl.pallas_call(
        functools.partial(_qkv_body, tq=tq, tiles_per_seq=S // tq),
        grid=(n_tiles,),
        in_specs=[main, prev, nxt,
                  pl.BlockSpec((4, D_MODEL), lambda i: (0, 0)),
                  pl.BlockSpec((1, D_MODEL), lambda i: (0, 0)),
                  wspec, wspec, wspec],
        out_specs=[
            pl.BlockSpec((tq, D_MODEL), lambda i: (i, 0)),
            pl.BlockSpec((D_MODEL, tq), lambda i: (0, i)),
            pl.BlockSpec((tq, D_MODEL), lambda i: (i, 0)),
        ],
        out_shape=[
            jax.ShapeDtypeStruct((T, D_MODEL), BF16),
            jax.ShapeDtypeStruct((D_MODEL, T), BF16),
            jax.ShapeDtypeStruct((T, D_MODEL), BF16),
        ],
        scratch_shapes=[pltpu.VMEM((tq + 2 * HALO, D_MODEL), F32)],
        compiler_params=_cparams(("parallel",)),
        name="mlstm_qkv",
    )(proj, proj, proj, cw, cb, wq, wk, wv)


def _mlstm_body(qf, ktf, vf, gf, qb, ktb, vb, gb, hf_ref, hb_ref, c_scr, n_scr, m_scr, *, L):
    @pl.when(pl.program_id(1) == 0)
    def _():
        c_scr[...] = jnp.zeros_like(c_scr)
        n_scr[...] = jnp.zeros_like(n_scr)
        m_scr[...] = jnp.zeros_like(m_scr)

    row = lax.broadcasted_iota(jnp.int32, (L, L), 0)
    col = lax.broadcasted_iota(jnp.int32, (L, L), 1)
    ones = jnp.ones((L, LANES), BF16)
    dirs = ((qf, ktf, vf, gf, hf_ref, False), (qb, ktb, vb, gb, hb_ref, True))
    chains = []
    for q_ref, kt_ref, v_ref, g_ref, h_ref, rev in dirs:
        G = g_ref[...]
        mask = (row <= col) if rev else (row >= col)
        cum = jnp.dot(mask.astype(F32), _log_sigmoid(G), precision=lax.Precision.HIGHEST,
                      preferred_element_type=F32)
        GT = G.T
        cumT = cum.T
        for h in range(N_HEADS):
            icol = (2 * N_HEADS if rev else 0) + h
            fcol = icol + N_HEADS
            b_col = cum[:, fcol:fcol + 1]
            chains.append(dict(
                hd=h + (N_HEADS if rev else 0), sl=slice(h * HEAD_DIM, (h + 1) * HEAD_DIM),
                q_ref=q_ref, kt_ref=kt_ref, v_ref=v_ref, h_ref=h_ref, mask=mask, b_col=b_col,
                r_row=GT[icol:icol + 1, :] - cumT[fcol:fcol + 1, :],
                g=b_col[0:1, :] if rev else b_col[L - 1:L, :]))

    for c in chains:
        m = m_scr[c["hd"]][:, 0:1]
        d_log = jnp.where(c["mask"], c["b_col"] + c["r_row"], NEG_INF)
        inter = c["b_col"] + m
        m_t = jnp.maximum(inter, jnp.max(d_log, axis=-1, keepdims=True))
        w_row = c["g"] + c["r_row"]
        m_new = jnp.maximum(c["g"] + m, jnp.max(w_row, axis=-1, keepdims=True))
        c.update(m=m, d_log=d_log, inter=inter, m_t=m_t, w_row=w_row, m_new=m_new)
    for c in chains:
        c["w_intra"] = jnp.exp(c["d_log"] - c["m_t"])
        c["w_inter"] = jnp.broadcast_to(jnp.exp(c["inter"] - c["m_t"]), (L, LANES))
        c["floor"] = jnp.broadcast_to(jnp.exp(-c["m_t"]), (L, LANES))
        c["decay"] = jnp.exp(c["g"] + c["m"] - c["m_new"])
        c["w_state"] = jnp.exp(c["w_row"] - c["m_new"])
    for c in chains:
        qc = c["q_ref"][:, c["sl"]]
        c["qc"] = qc
        c["s"] = (jnp.dot(qc, c["kt_ref"][c["sl"], :], preferred_element_type=F32)
                  * c["w_intra"]).astype(BF16)
        c["qC"] = jnp.dot(qc, c_scr[c["hd"]].astype(BF16), preferred_element_type=F32)
        c["qn"] = jnp.dot(qc, n_scr[c["hd"]].astype(BF16), preferred_element_type=F32)
    for c in chains:
        vc = c["v_ref"][:, c["sl"]]
        sv = jnp.dot(c["s"], vc, preferred_element_type=F32)
        s_sum = jnp.dot(c["s"], ones, preferred_element_type=F32)
        den = c["w_inter"] * c["qn"] + s_sum
        inv = 1.0 / jnp.maximum(jnp.abs(den), c["floor"])
        two = lambda a: jnp.concatenate([a] * (HEAD_DIM // LANES), axis=-1)
        c["h_ref"][:, c["sl"]] = (two(c["w_inter"]) * c["qC"] + sv) * two(inv)
    for c in chains:
        hd = c["hd"]
        kwt = (c["kt_ref"][c["sl"], :].astype(F32) * c["w_state"]).astype(BF16)
        vc = c["v_ref"][:, c["sl"]]
        c_scr[hd] = c["decay"] * c_scr[hd] + jnp.dot(kwt, vc, preferred_element_type=F32)
        n_scr[hd] = c["decay"] * n_scr[hd] + jnp.dot(kwt, ones, preferred_element_type=F32)
        m_scr[hd] = jnp.broadcast_to(c["m_new"], (1, LANES))


def _mlstm_scan(q, kt, v, gates, *, B, S, L):
    T = B * S
    nc = S // L
    fwd = lambda b, c: b * nc + c
    bwd = lambda b, c: b * nc + (nc - 1 - c)

    def specs(idx):
        return [
            pl.BlockSpec((L, D_MODEL), lambda b, c: (idx(b, c), 0)),
            pl.BlockSpec((D_MODEL, L), lambda b, c: (0, idx(b, c))),
            pl.BlockSpec((L, D_MODEL), lambda b, c: (idx(b, c), 0)),
            pl.BlockSpec((L, LANES), lambda b, c: (idx(b, c), 0)),
        ]

    return pl.pallas_call(
        functools.partial(_mlstm_body, L=L),
        grid=(B, nc),
        in_specs=specs(fwd) + specs(bwd),
        out_specs=[
            pl.BlockSpec((L, D_MODEL), lambda b, c: (fwd(b, c), 0)),
            pl.BlockSpec((L, D_MODEL), lambda b, c: (bwd(b, c), 0)),
        ],
        out_shape=[jax.ShapeDtypeStruct((T, D_MODEL), F32)] * 2,
        scratch_shapes=[
            pltpu.VMEM((2 * N_HEADS, HEAD_DIM, HEAD_DIM), F32),
            pltpu.VMEM((2 * N_HEADS, HEAD_DIM, LANES), F32),
            pltpu.VMEM((2 * N_HEADS, 1, LANES), F32),
        ],
        compiler_params=_cparams(("parallel", "arbitrary")),
        name="mlstm_scan",
    )(q, kt, v, gates, q, kt, v, gates)


def _lru_dir(x, prev, nxt, is_first, is_last, cw_ref, cb_ref, wr_ref, br_ref, wi_ref, bi_ref,
             lam_ref, h_ref, ext_scr, a_scr, u_scr, carry_scr, *, ts, d, rev):
    xc = _conv_tile(x, prev, nxt, is_first, is_last, cw_ref, cb_ref, ext_scr, ts)
    xcb = xc.astype(BF16)
    lam = lam_ref[d:d + 1, :]
    sp = jnp.maximum(-lam, 0.0) + jnp.log(1.0 + jnp.exp(-jnp.abs(lam)))
    for n in range(LRU_BLOCKS):
        sl = slice(n * LRU_BLOCK_DIM, (n + 1) * LRU_BLOCK_DIM)
        xb = xcb[:, sl]
        r = _sigmoid(jnp.dot(xb, wr_ref[d, n], preferred_element_type=F32) + br_ref[d:d + 1, sl])
        ig = _sigmoid(jnp.dot(xb, wi_ref[d, n], preferred_element_type=F32) + bi_ref[d:d + 1, sl])
        a = jnp.exp(-LRU_C * r * sp[:, sl])
        a_scr[:, sl] = a
        u_scr[:, sl] = jnp.sqrt(1.0 - a * a) * (ig * xc[:, sl])

    sub = lax.broadcasted_iota(jnp.int32, (ts, D_MODEL), 0) % SUBLANES
    A = a_scr[...]
    U = u_scr[...]
    for step in (1, 2, 4):
        if rev:
            a_sh = pltpu.roll(A, ts - step, axis=0)
            u_sh = pltpu.roll(U, ts - step, axis=0)
            ok = sub < SUBLANES - step
        else:
            a_sh = pltpu.roll(A, step, axis=0)
            u_sh = pltpu.roll(U, step, axis=0)
            ok = sub >= step
        U = jnp.where(ok, A * u_sh + U, U)
        A = jnp.where(ok, A * a_sh, A)
    a_scr[...] = A
    u_scr[...] = U

    n_grp = ts // SUBLANES

    def body(k, carry):
        g = (n_grp - 1 - k) if rev else k
        r0 = pl.multiple_of(g * SUBLANES, SUBLANES)
        hg = a_scr[pl.ds(r0, SUBLANES), :] * carry + u_scr[pl.ds(r0, SUBLANES), :]
        h_ref[pl.ds(r0, SUBLANES), :] = hg
        return hg[0:1, :] if rev else hg[SUBLANES - 1:SUBLANES, :]

    carry_scr[d:d + 1, :] = lax.fori_loop(0, n_grp, body, carry_scr[d:d + 1, :])


def _lru_body(xf, pf, nf, xb, pb, nb, cw_ref, cb_ref, wr_ref, br_ref, wi_ref, bi_ref, lam_ref,
              hf_ref, hb_ref, ext_scr, a_scr, u_scr, carry_scr, *, ts, nt):
    i = pl.program_id(1)

    @pl.when(i == 0)
    def _():
        carry_scr[...] = jnp.zeros_like(carry_scr)

    common = (cw_ref, cb_ref, wr_ref, br_ref, wi_ref, bi_ref, lam_ref)
    _lru_dir(xf[...], pf[...], nf[...], i == 0, i == nt - 1, *common, hf_ref,
             ext_scr, a_scr, u_scr, carry_scr, ts=ts, d=0, rev=False)
    _lru_dir(xb[...], pb[...], nb[...], i == nt - 1, i == 0, *common, hb_ref,
             ext_scr, a_scr, u_scr, carry_scr, ts=ts, d=1, rev=True)


def _lru(proj, cw, cb, wr, br, wi, bi, lam, *, B, S, ts):
    T = B * S
    nt = S // ts
    fwd = lambda b, i: b * nt + i
    bwd = lambda b, i: b * nt + (nt - 1 - i)
    mk = _halo_specs(2, ts, B * nt)
    full = lambda shape: pl.BlockSpec(shape, lambda b, i: (0,) * len(shape))
    return pl.pallas_call(
        functools.partial(_lru_body, ts=ts, nt=nt),
        grid=(B, nt),
        in_specs=mk(fwd) + mk(bwd) + [
            full((4, D_MODEL)), full((1, D_MODEL)),
            full((2, LRU_BLOCKS, LRU_BLOCK_DIM, LRU_BLOCK_DIM)), full((2, D_MODEL)),
            full((2, LRU_BLOCKS, LRU_BLOCK_DIM, LRU_BLOCK_DIM)), full((2, D_MODEL)),
            full((2, D_MODEL)),
        ],
        out_specs=[
            pl.BlockSpec((ts, D_MODEL), lambda b, i: (fwd(b, i), 0)),
            pl.BlockSpec((ts, D_MODEL), lambda b, i: (bwd(b, i), 0)),
        ],
        out_shape=[jax.ShapeDtypeStruct((T, D_MODEL), F32)] * 2,
        scratch_shapes=[
            pltpu.VMEM((ts + 2 * HALO, D_MODEL), F32),
            pltpu.VMEM((ts, D_MODEL), F32),
            pltpu.VMEM((ts, D_MODEL), F32),
            pltpu.VMEM((2, D_MODEL), F32),
        ],
        compiler_params=_cparams(("parallel", "arbitrary")),
        name="rglru",
    )(proj, proj, proj, proj, proj, proj, cw, cb, wr, br, wi, bi, lam)


def _merge_body(x_ref, hmf_ref, hmb_ref, opre_ref, hlf_ref, hlb_ref, lgate_ref, gm_ref, gl_ref,
                ng_ref, wbm_ref, wbl_ref, wout_ref, g2_ref, x1_ref, h2t_ref):
    hm = hmf_ref[...] + hmb_ref[...]
    parts = []
    for h in range(N_HEADS):
        sl = slice(h * HEAD_DIM, (h + 1) * HEAD_DIM)
        hh = hm[:, sl]
        ms = jnp.mean(hh * hh, axis=-1, keepdims=True)
        parts.append(hh * lax.rsqrt(ms + EPS) * ng_ref[:, sl])
    hn = jnp.concatenate(parts, axis=-1)
    y_m = (_sigmoid(opre_ref[...]) * hn).astype(BF16)
    y_l = ((hlf_ref[...] + hlb_ref[...]) * _gelu(lgate_ref[...])).astype(BF16)
    merged = (_sigmoid(gm_ref[...]) * jnp.dot(y_m, wbm_ref[...], preferred_element_type=F32)
              + _sigmoid(gl_ref[...]) * jnp.dot(y_l, wbl_ref[...], preferred_element_type=F32))
    x1 = x_ref[...] + jnp.dot(merged.astype(BF16), wout_ref[...], preferred_element_type=F32)
    x1_ref[...] = x1
    ms = jnp.mean(x1 * x1, axis=-1, keepdims=True)
    h2 = x1 * lax.rsqrt(ms + EPS) * g2_ref[...]
    h2t_ref[...] = h2.T.astype(BF16)


def _merge(x2d, hmf, hmb, proj, hlf, hlb, ng, wbm, wbl, wout, g2, *, tm):
    T = x2d.shape[0]
    tile = lambda col: pl.BlockSpec((tm, D_MODEL), lambda i: (i, col))
    vec = pl.BlockSpec((1, D_MODEL), lambda i: (0, 0))
    mat = pl.BlockSpec((D_MODEL, D_MODEL), lambda i: (0, 0))
    return pl.pallas_call(
        _merge_body,
        grid=(T // tm,),
        in_specs=[tile(0), tile(0), tile(0), tile(1), tile(0), tile(0), tile(3), tile(4), tile(5),
                  vec, mat, mat, mat, vec],
        out_specs=[
            pl.BlockSpec((tm, D_MODEL), lambda i: (i, 0)),
            pl.BlockSpec((D_MODEL, tm), lambda i: (0, i)),
        ],
        out_shape=[
            jax.ShapeDtypeStruct((T, D_MODEL), F32),
            jax.ShapeDtypeStruct((D_MODEL, T), BF16),
        ],
        compiler_params=_cparams(("parallel",)),
        name="merge",
    )(x2d, hmf, hmb, proj, hlf, hlb, proj, proj, proj, ng, wbm, wbl, wout, g2)


def _top16_ranked(s):
    n, tb = s.shape
    iota = lax.broadcasted_iota(jnp.int32, (n, tb), 0).astype(F32)
    rank = jnp.full((n, tb), float(PEER_TOPK), F32)
    vals = []
    for j in range(PEER_TOPK):
        m = jnp.max(s, axis=0, keepdims=True)
        idx = jnp.min(jnp.where(s == m, iota, float(n)), axis=0, keepdims=True)
        sel = iota == idx
        rank = jnp.where(sel, float(j), rank)
        s = jnp.where(sel, NEG_INF, s)
        vals.append(m)
    return vals, rank


CODE_BASE = 2.0 ** 100
CODE_STEP = 2.0 ** 96
CODE_TEST = -(2.0 ** 99)


def _extract16_coded(x):
    vals = []
    for j in range(PEER_TOPK):
        m = jnp.max(x, axis=0, keepdims=True)
        x = jnp.where(x == m, -(CODE_BASE + j * CODE_STEP), x)
        vals.append(m)
    return vals, x


def _top16_ranked_notie(s):
    vals, coded = _extract16_coded(s)
    taken = coded < CODE_TEST
    rank = jnp.where(taken, (-coded - CODE_BASE) * (1.0 / CODE_STEP), float(PEER_TOPK))
    count = jnp.sum(jnp.where(taken, 1.0, 0.0), axis=0, keepdims=True)
    return vals, rank, count


_CAND_ROW_GROUPS = [(0, 0, 8), (0, 8, 8), (1, 0, 8), (2, 0, 5), (3, 0, 4), (4, 0, 3),
                    (5, 0, 2), (6, 0, 2), (7, 0, 2)]


def _pair_candidates(v1, v2):
    tb = v1[0].shape[1]
    v1_hi = jnp.concatenate(v1[8:], axis=0)
    v2_lo = jnp.concatenate(v2[:8], axis=0)
    v2_hi = jnp.concatenate(v2[8:], axis=0)
    sub = lax.broadcasted_iota(jnp.int32, (SUBLANES, tb), 0)
    cands, poss, valids = [], [], []
    for j1, base, nv in _CAND_ROW_GROUPS:
        c = v1[j1] + (v2_lo if base == 0 else v2_hi)
        cands.append(jnp.where(sub < nv, c, NEG_INF))
        poss.append((sub + (j1 * PEER_TOPK + base)).astype(F32))
        valids.append(jnp.where(sub < nv, 1.0, 0.0))
    cands.append(v1_hi + v2[0])
    poss.append(((sub + SUBLANES) * PEER_TOPK).astype(F32))
    valids.append(jnp.ones((SUBLANES, tb), F32))
    return (jnp.concatenate(cands, axis=0), jnp.concatenate(poss, axis=0),
            jnp.concatenate(valids, axis=0))


def _staircase(taken, cand, cmax):
    Z = jnp.sum(taken * jnp.exp(cand - cmax), axis=0, keepdims=True)
    grp = lambda g: taken[g * SUBLANES:(g + 1) * SUBLANES, :]
    L = [jnp.sum(grp(0) + grp(1), axis=0, keepdims=True)]
    for g in range(2, 9):
        L.append(jnp.sum(grp(g), axis=0, keepdims=True))
    last = grp(9)
    for k in range(SUBLANES):
        L.append(last[k:k + 1, :])
    return L, Z


def _select_pairs(v1, v2):
    cand0, pos, _ = _pair_candidates(v1, v2)
    cand = cand0
    taken = jnp.zeros_like(cand)
    big = float(PEER_TOPK * PEER_TOPK)
    for _ in range(PEER_TOPK):
        m = jnp.max(cand, axis=0, keepdims=True)
        p = jnp.min(jnp.where(cand == m, pos, big), axis=0, keepdims=True)
        sel = pos == p
        taken = jnp.where(sel, 1.0, taken)
        cand = jnp.where(sel, NEG_INF, cand)
    return _staircase(taken, cand0, v1[0] + v2[0])


def _select_pairs_notie(v1, v2):
    cand0, _, valid = _pair_candidates(v1, v2)
    _, coded = _extract16_coded(cand0)
    taken = jnp.where(coded < CODE_TEST, valid, 0.0)
    L, Z = _staircase(taken, cand0, v1[0] + v2[0])
    return L, Z, jnp.sum(taken, axis=0, keepdims=True)


def _store_selection(h, s1, s2, v1, rank1, v2, rank2, L, Z, r2_scr, e2_scr, lim_scr, coef_scr):
    lim = jnp.full_like(rank1, -0.5)
    for j1 in reversed(range(PEER_TOPK)):
        lim = jnp.where(rank1 < j1 + 0.5, L[j1] - 0.5, lim)
    r2_scr[h] = rank2.astype(BF16)
    e2_scr[h] = jnp.exp(s2 - v2[0]).astype(BF16)
    lim_scr[h] = lim
    coef_scr[h] = jnp.exp(s1 - v1[0]) / Z


MXU_COLS = 256


def _peer_body(h2t_ref, x1_ref, wqt_ref, keys_ref, u0_ref, u_ref, vt_ref, vtl_ref, gf_ref, y_ref,
               r2_scr, e2_scr, lim_scr, coef_scr, limb_scr, coefb_scr, act_scr, wa_scr, acc_scr,
               *, rows):
    j = pl.program_id(1)
    last = pl.num_programs(1) - 1
    tb = h2t_ref.shape[1]
    cur = j % 2
    oth = 1 - cur
    halves = [slice(c * MXU_COLS, (c + 1) * MXU_COLS) for c in range(tb // MXU_COLS)]

    def act_half(u_blk, cols):
        a = jnp.dot(u_blk[...], h2t_ref[:, cols], preferred_element_type=F32)
        return _gelu(a).astype(BF16)

    @pl.when(j == 0)
    def _():
        h2t = h2t_ref[...]
        for h in range(PEER_HEADS):
            qt = jnp.dot(wqt_ref[h * 2 * PEER_HALF:(h + 1) * 2 * PEER_HALF, :], h2t,
                         preferred_element_type=F32).astype(BF16)
            s1 = jnp.dot(keys_ref[2 * h], qt[:PEER_HALF], preferred_element_type=F32)
            s2 = jnp.dot(keys_ref[2 * h + 1], qt[PEER_HALF:], preferred_element_type=F32)
            out = (r2_scr, e2_scr, lim_scr, coef_scr)
            v1, rank1, n1 = _top16_ranked_notie(s1)
            v2, rank2, n2 = _top16_ranked_notie(s2)
            L, Z, n3 = _select_pairs_notie(v1, v2)
            _store_selection(h, s1, s2, v1, rank1, v2, rank2, L, Z, *out)
            k = float(PEER_TOPK)
            miscount = jnp.max(jnp.abs(n1 - k) + jnp.abs(n2 - k) + jnp.abs(n3 - k))

            @pl.when(miscount > 0.0)
            def _():
                v1, rank1 = _top16_ranked(s1)
                v2, rank2 = _top16_ranked(s2)
                L, Z = _select_pairs(v1, v2)
                _store_selection(h, s1, s2, v1, rank1, v2, rank2, L, Z, *out)
        acc_scr[...] = jnp.zeros_like(acc_scr)
        wa_scr[1] = jnp.zeros(wa_scr.shape[1:], BF16)
        for cols in halves:
            act_scr[0, :, cols] = act_half(u0_ref, cols)

    for h in range(PEER_HEADS):
        for ii in range(rows):
            i1 = j * rows + ii
            lim = jnp.broadcast_to(lim_scr[h, pl.ds(i1, 1), :], (BF16_ROWS, tb))
            coef = jnp.broadcast_to(coef_scr[h, pl.ds(i1, 1), :], (BF16_ROWS, tb))
            limb_scr[h * rows + ii] = lim.astype(BF16)
            coefb_scr[h * rows + ii] = coef.astype(BF16)
    def next_act(c):
        act_scr[oth, :, halves[c]] = act_half(u_ref, halves[c])

    def prev_out(c):
        acc_scr[:, halves[c]] += jnp.dot(vt_ref[...], wa_scr[oth, :, halves[c]],
                                         preferred_element_type=F32)

    n_lt = tb // LANES
    mxu_pieces = ([functools.partial(next_act, c) for c in range(len(halves))]
                  + [functools.partial(prev_out, c) for c in range(len(halves))])
    for lt in range(n_lt):
        ls = slice(lt * LANES, (lt + 1) * LANES)
        for sb in range(PEER_KEYS // BF16_ROWS):
            ss = slice(sb * BF16_ROWS, (sb + 1) * BF16_ROWS)
            accs = [None] * rows
            for h in range(PEER_HEADS):
                r2 = r2_scr[h, ss, ls]
                e2 = e2_scr[h, ss, ls]
                for ii in range(rows):
                    term = jnp.where(r2 < limb_scr[h * rows + ii, :, ls],
                                     e2 * coefb_scr[h * rows + ii, :, ls], 0.0)
                    accs[ii] = term if h == 0 else accs[ii] + term
            for ii in range(rows):
                rs = slice(ii * PEER_KEYS + sb * BF16_ROWS, ii * PEER_KEYS + (sb + 1) * BF16_ROWS)
                wa_scr[cur, rs, ls] = accs[ii] * act_scr[cur, rs, ls]
        for piece in mxu_pieces[lt * len(mxu_pieces) // n_lt:(lt + 1) * len(mxu_pieces) // n_lt]:
            piece()

    @pl.when(j == last)
    def _():
        acc = acc_scr[...] + jnp.dot(vtl_ref[...], wa_scr[cur], preferred_element_type=F32)
        x2 = x1_ref[...] + acc.T
        ms = jnp.mean(x2 * x2, axis=-1, keepdims=True)
        y_ref[...] = x2 * lax.rsqrt(ms + EPS) * gf_ref[...]


def _peer(h2t, x1, wqt, keys, u, vt, gf, *, tb, ec):
    T = x1.shape[0]
    n_chunks = u.shape[0] // ec
    n_q = wqt.shape[0]
    sel = lambda dt: pltpu.VMEM((PEER_HEADS, PEER_KEYS, tb), dt)
    rowb = lambda: pltpu.VMEM((PEER_HEADS * (ec // PEER_KEYS), BF16_ROWS, tb), BF16)
    once = dict(pipeline_mode=pl.Buffered(1))
    return pl.pallas_call(
        functools.partial(_peer_body, rows=ec // PEER_KEYS),
        grid=(T // tb, n_chunks),
        in_specs=[
            pl.BlockSpec((D_MODEL, tb), lambda i, j: (0, i)),
            pl.BlockSpec((tb, D_MODEL), lambda i, j: (i, 0)),
            pl.BlockSpec((n_q, D_MODEL), lambda i, j: (0, 0), **once),
            pl.BlockSpec((2 * PEER_HEADS, PEER_KEYS, PEER_HALF), lambda i, j: (0, 0, 0), **once),
            pl.BlockSpec((ec, D_MODEL), lambda i, j: (0, 0), **once),
            pl.BlockSpec((ec, D_MODEL), lambda i, j: (jnp.minimum(j + 1, n_chunks - 1), 0)),
            pl.BlockSpec((D_MODEL, ec), lambda i, j: (0, jnp.maximum(j - 1, 0))),
            pl.BlockSpec((D_MODEL, ec), lambda i, j: (0, n_chunks - 1), **once),
            pl.BlockSpec((1, D_MODEL), lambda i, j: (0, 0), **once),
        ],
        out_specs=pl.BlockSpec((tb, D_MODEL), lambda i, j: (i, 0)),
        out_shape=jax.ShapeDtypeStruct((T, D_MODEL), F32),
        scratch_shapes=[sel(BF16), sel(BF16), sel(F32), sel(F32), rowb(), rowb(),
                        pltpu.VMEM((2, ec, tb), BF16),
                        pltpu.VMEM((2, ec, tb), BF16),
                        pltpu.VMEM((D_MODEL, tb), F32)],
        compiler_params=_cparams(("parallel", "arbitrary")),
        name="peer",
    )(h2t, x1, wqt, keys, u, u, vt, vt, gf)


def _layer(x, p, *, tm_in, tq, L, ts, tm_merge, tb, ec):
    B, S, D = x.shape
    T = B * S
    x2d = x.reshape(T, D)
    w_in = p["w_in"]
    g0 = 2 * D_MODEL
    w_main = jnp.concatenate([w_in[:, :g0], w_in[:, g0 + N_GATES:]], axis=1).astype(BF16)
    w_gate = jnp.pad(w_in[:, g0:g0 + N_GATES], ((0, 0), (0, LANES - N_GATES))).astype(BF16)
    b_gate = jnp.pad(p["b_gates"], (0, LANES - N_GATES)).reshape(1, LANES)
    row = lambda a: a.reshape(1, -1)

    proj, gates = _in_proj(x2d, row(p["norm1_g"]), w_main, w_gate, b_gate, tm=tm_in)
    q, kt, v = _qkv(proj, p["mlstm_conv_w"], row(p["mlstm_conv_b"]),
                    p["mlstm_w_q"].astype(BF16), p["mlstm_w_k"].astype(BF16),
                    p["mlstm_w_v"].astype(BF16), S=S, tq=tq)
    hmf, hmb = _mlstm_scan(q, kt, v, gates, B=B, S=S, L=L)
    hlf, hlb = _lru(proj, p["lru_conv_w"], row(p["lru_conv_b"]),
                    p["lru_w_r"].astype(BF16), p["lru_b_r"], p["lru_w_i"].astype(BF16),
                    p["lru_b_i"], p["lru_lambda"], B=B, S=S, ts=ts)
    x1, h2t = _merge(x2d, hmf, hmb, proj, hlf, hlb, row(p["mlstm_norm_g"]),
                     p["w_branch_mlstm"].astype(BF16), p["w_branch_lru"].astype(BF16),
                     p["w_out"].astype(BF16), row(p["norm2_g"]), tm=tm_merge)
    n_keys = PEER_HEADS * 2
    keys = p["peer_sub_keys"].reshape(n_keys, PEER_KEYS, PEER_HALF).astype(BF16)
    y = _peer(h2t, x1, p["peer_w_q"].T.astype(BF16), keys, p["peer_u"].astype(BF16),
              p["peer_v"].T.astype(BF16), row(p["final_norm_g"]), tb=tb, ec=ec)
    return y.reshape(B, S, D)


def kernel(x, norm1_g, w_in, b_gates, mlstm_conv_w, mlstm_conv_b, mlstm_w_q, mlstm_w_k, mlstm_w_v,
           mlstm_norm_g, lru_conv_w, lru_conv_b, lru_w_r, lru_b_r, lru_w_i, lru_b_i, lru_lambda,
           w_branch_mlstm, w_branch_lru, w_out, norm2_g, peer_w_q, peer_sub_keys, peer_u, peer_v,
           final_norm_g):
    p = dict(norm1_g=norm1_g[0], w_in=w_in[0], b_gates=b_gates[0], mlstm_conv_w=mlstm_conv_w[0],
             mlstm_conv_b=mlstm_conv_b[0], mlstm_w_q=mlstm_w_q[0], mlstm_w_k=mlstm_w_k[0],
             mlstm_w_v=mlstm_w_v[0], mlstm_norm_g=mlstm_norm_g[0], lru_conv_w=lru_conv_w[0],
             lru_conv_b=lru_conv_b[0], lru_w_r=lru_w_r[0], lru_b_r=lru_b_r[0], lru_w_i=lru_w_i[0],
             lru_b_i=lru_b_i[0], lru_lambda=lru_lambda[0], w_branch_mlstm=w_branch_mlstm[0],
             w_branch_lru=w_branch_lru[0], w_out=w_out[0], norm2_g=norm2_g[0],
             peer_w_q=peer_w_q[0], peer_sub_keys=peer_sub_keys[0], peer_u=peer_u[0],
             peer_v=peer_v[0], final_norm_g=final_norm_g)
    return _layer(x, p, tm_in=1024, tq=512, L=128, ts=512, tm_merge=256, tb=512, ec=1024)
```

```python
import functools
import math

import jax
import jax.numpy as jnp
from jax import lax
from jax.experimental import pallas as pl
from jax.experimental.pallas import tpu as pltpu

F32 = jnp.float32
BF16 = jnp.bfloat16

D_MODEL = 1024
N_HEADS = 4
HEAD_DIM = 256
N_GATES = 16
LRU_BLOCKS = 8
LRU_BLOCK_DIM = 128
LRU_C = 8.0
PEER_HEADS = 8
PEER_KEYS = 128
PEER_HALF = 128
PEER_TOPK = 16
EPS = 1e-6
LANES = 128
SUBLANES = 8
BF16_ROWS = 16
HALO = SUBLANES
NEG_INF = float("-inf")

VMEM_LIMIT = 56 * 1024 * 1024


def _cparams(sem, flags=None):
    return pltpu.CompilerParams(dimension_semantics=sem, vmem_limit_bytes=VMEM_LIMIT, flags=flags)


def _sigmoid(x):
    return 0.5 * jnp.tanh(0.5 * x) + 0.5


def _gelu(x):
    return 0.5 * x * (1.0 + lax.erf(x * (1.0 / math.sqrt(2.0))))


def _log_sigmoid(x):
    return jnp.minimum(x, 0.0) - jnp.log(1.0 + jnp.exp(-jnp.abs(x)))


def _inproj_body(x_ref, g_ref, w_ref, wg_ref, bg_ref, o_ref, gate_ref, h_scr):
    @pl.when(pl.program_id(1) == 0)
    def _():
        x = x_ref[...]
        ms = jnp.mean(x * x, axis=-1, keepdims=True)
        hb = (x * lax.rsqrt(ms + EPS) * g_ref[...]).astype(BF16)
        h_scr[...] = hb
        gate_ref[...] = jnp.dot(hb, wg_ref[...], preferred_element_type=F32) + bg_ref[...]

    o_ref[...] = jnp.dot(h_scr[...], w_ref[...], preferred_element_type=F32)


def _in_proj(x2d, g1, w_main, w_gate, b_gate, *, tm):
    T = x2d.shape[0]
    n_col = w_main.shape[1] // D_MODEL
    return pl.pallas_call(
        _inproj_body,
        grid=(T // tm, n_col),
        in_specs=[
            pl.BlockSpec((tm, D_MODEL), lambda i, j: (i, 0)),
            pl.BlockSpec((1, D_MODEL), lambda i, j: (0, 0)),
            pl.BlockSpec((D_MODEL, D_MODEL), lambda i, j: (0, j)),
            pl.BlockSpec((D_MODEL, LANES), lambda i, j: (0, 0)),
            pl.BlockSpec((1, LANES), lambda i, j: (0, 0)),
        ],
        out_specs=[
            pl.BlockSpec((tm, D_MODEL), lambda i, j: (i, j)),
            pl.BlockSpec((tm, LANES), lambda i, j: (i, 0)),
        ],
        out_shape=[
            jax.ShapeDtypeStruct((T, n_col * D_MODEL), F32),
            jax.ShapeDtypeStruct((T, LANES), F32),
        ],
        scratch_shapes=[pltpu.VMEM((tm, D_MODEL), BF16)],
        compiler_params=_cparams(("parallel", "arbitrary")),
        name="in_proj",
    )(x2d, g1, w_main, w_gate, b_gate)


def _conv_tile(x, prev, nxt, is_first, is_last, cw_ref, cb_ref, ext_scr, tq):
    ext_scr[0:HALO, :] = jnp.where(is_first, 0.0, prev)
    ext_scr[HALO:HALO + tq, :] = x
    ext_scr[HALO + tq:2 * HALO + tq, :] = jnp.where(is_last, 0.0, nxt)
    acc = cb_ref[...] + cw_ref[0:1, :] * ext_scr[HALO - 2:HALO - 2 + tq, :]
    for j in range(1, 4):
        acc = acc + cw_ref[j:j + 1, :] * ext_scr[HALO - 2 + j:HALO - 2 + j + tq, :]
    return acc


def _halo_specs(col, tq, n_tiles):
    r = tq // HALO

    def specs(tile_of):
        return [
            pl.BlockSpec((tq, D_MODEL), lambda *a: (tile_of(*a), col)),
            pl.BlockSpec((HALO, D_MODEL), lambda *a: (jnp.maximum(tile_of(*a) * r - 1, 0), col)),
            pl.BlockSpec((HALO, D_MODEL),
                         lambda *a: (jnp.minimum((tile_of(*a) + 1) * r, n_tiles * r - 1), col)),
        ]

    return specs


def _qkv_body(xm_ref, prev_ref, next_ref, cw_ref, cb_ref, wq_ref, wk_ref, wv_ref,
              q_ref, kt_ref, v_ref, ext_scr, *, tq, tiles_per_seq):
    i = pl.program_id(0)
    pos = i % tiles_per_seq
    xm = xm_ref[...]
    acc = _conv_tile(xm, prev_ref[...], next_ref[...], pos == 0, pos == tiles_per_seq - 1,
                     cw_ref, cb_ref, ext_scr, tq)
    xc = acc * _sigmoid(acc)
    for h in range(N_HEADS):
        sl = slice(h * HEAD_DIM, (h + 1) * HEAD_DIM)
        xch = xc[:, sl].astype(BF16)
        q_ref[:, sl] = jnp.dot(xch, wq_ref[h], preferred_element_type=F32).astype(BF16)
        kh = jnp.dot(xch, wk_ref[h], preferred_element_type=F32) * (1.0 / math.sqrt(HEAD_DIM))
        kt_ref[sl, :] = kh.T.astype(BF16)
        v_ref[:, sl] = jnp.dot(xm[:, sl].astype(BF16), wv_ref[h],
                               preferred_element_type=F32).astype(BF16)


def _qkv(proj, cw, cb, wq, wk, wv, *, S, tq):
    T = proj.shape[0]
    n_tiles = T // tq
    main, prev, nxt = _halo_specs(0, tq, n_tiles)(lambda i: i)
    wspec = pl.BlockSpec((N_HEADS, HEAD_DIM, HEAD_DIM), lambda i: (0, 0, 0))
    return pl.pallas_call(
        functools.partial(_qkv_body, tq=tq, tiles_per_seq=S // tq),
        grid=(n_tiles,),
        in_specs=[main, prev, nxt,
                  pl.BlockSpec((4, D_MODEL), lambda i: (0, 0)),
                  pl.BlockSpec((1, D_MODEL), lambda i: (0, 0)),
                  wspec, wspec, wspec],
        out_specs=[
            pl.BlockSpec((tq, D_MODEL), lambda i: (i, 0)),
            pl.BlockSpec((D_MODEL, tq), lambda i: (0, i)),
            pl.BlockSpec((tq, D_MODEL), lambda i: (i, 0)),
        ],
        out_shape=[
            jax.ShapeDtypeStruct((T, D_MODEL), BF16),
            jax.ShapeDtypeStruct((D_MODEL, T), BF16),
            jax.ShapeDtypeStruct((T, D_MODEL), BF16),
        ],
        scratch_shapes=[pltpu.VMEM((tq + 2 * HALO, D_MODEL), F32)],
        compiler_params=_cparams(("parallel",)),
        name="mlstm_qkv",
    )(proj, proj, proj, cw, cb, wq, wk, wv)


def _mlstm_body(qf, ktf, vf, gf, qb, ktb, vb, gb, hf_ref, hb_ref, c_scr, n_scr, m_scr, *, L):
    @pl.when(pl.program_id(1) == 0)
    def _():
        c_scr[...] = jnp.zeros_like(c_scr)
        n_scr[...] = jnp.zeros_like(n_scr)
        m_scr[...] = jnp.zeros_like(m_scr)

    row = lax.broadcasted_iota(jnp.int32, (L, L), 0)
    col = lax.broadcasted_iota(jnp.int32, (L, L), 1)
    ones = jnp.ones((L, LANES), BF16)
    dirs = ((qf, ktf, vf, gf, hf_ref, False), (qb, ktb, vb, gb, hb_ref, True))
    chains = []
    for q_ref, kt_ref, v_ref, g_ref, h_ref, rev in dirs:
        G = g_ref[...]
        mask = (row <= col) if rev else (row >= col)
        cum = jnp.dot(mask.astype(F32), _log_sigmoid(G), precision=lax.Precision.HIGHEST,
                      preferred_element_type=F32)
        GT = G.T
        cumT = cum.T
        for h in range(N_HEADS):
            icol = (2 * N_HEADS if rev else 0) + h
            fcol = icol + N_HEADS
            b_col = cum[:, fcol:fcol + 1]
            chains.append(dict(
                hd=h + (N_HEADS if rev else 0), sl=slice(h * HEAD_DIM, (h + 1) * HEAD_DIM),
                q_ref=q_ref, kt_ref=kt_ref, v_ref=v_ref, h_ref=h_ref, mask=mask, b_col=b_col,
                r_row=GT[icol:icol + 1, :] - cumT[fcol:fcol + 1, :],
                g=b_col[0:1, :] if rev else b_col[L - 1:L, :]))

    for c in chains:
        m = m_scr[c["hd"]][:, 0:1]
        d_log = jnp.where(c["mask"], c["b_col"] + c["r_row"], NEG_INF)
        inter = c["b_col"] + m
        m_t = jnp.maximum(inter, jnp.max(d_log, axis=-1, keepdims=True))
        w_row = c["g"] + c["r_row"]
        m_new = jnp.maximum(c["g"] + m, jnp.max(w_row, axis=-1, keepdims=True))
        c.update(m=m, d_log=d_log, inter=inter, m_t=m_t, w_row=w_row, m_new=m_new)
    for c in chains:
        c["w_intra"] = jnp.exp(c["d_log"] - c["m_t"])
        c["w_inter"] = jnp.broadcast_to(jnp.exp(c["inter"] - c["m_t"]), (L, LANES))
        c["floor"] = jnp.broadcast_to(jnp.exp(-c["m_t"]), (L, LANES))
        c["decay"] = jnp.exp(c["g"] + c["m"] - c["m_new"])
        c["w_state"] = jnp.exp(c["w_row"] - c["m_new"])
    for c in chains:
        qc = c["q_ref"][:, c["sl"]]
        c["qc"] = qc
        c["s"] = (jnp.dot(qc, c["kt_ref"][c["sl"], :], preferred_element_type=F32)
                  * c["w_intra"]).astype(BF16)
        c["qC"] = jnp.dot(qc, c_scr[c["hd"]].astype(BF16), preferred_element_type=F32)
        c["qn"] = jnp.dot(qc, n_scr[c["hd"]].astype(BF16), preferred_element_type=F32)
    for c in chains:
        vc = c["v_ref"][:, c["sl"]]
        sv = jnp.dot(c["s"], vc, preferred_element_type=F32)
        s_sum = jnp.dot(c["s"], ones, preferred_element_type=F32)
        den = c["w_inter"] * c["qn"] + s_sum
        inv = 1.0 / jnp.maximum(jnp.abs(den), c["floor"])
        two = lambda a: jnp.concatenate([a] * (HEAD_DIM // LANES), axis=-1)
        c["h_ref"][:, c["sl"]] = (two(c["w_inter"]) * c["qC"] + sv) * two(inv)
    for c in chains:
        hd = c["hd"]
        kwt = (c["kt_ref"][c["sl"], :].astype(F32) * c["w_state"]).astype(BF16)
        vc = c["v_ref"][:, c["sl"]]
        c_scr[hd] = c["decay"] * c_scr[hd] + jnp.dot(kwt, vc, preferred_element_type=F32)
        n_scr[hd] = c["decay"] * n_scr[hd] + jnp.dot(kwt, ones, preferred_element_type=F32)
        m_scr[hd] = jnp.broadcast_to(c["m_new"], (1, LANES))


def _mlstm_scan(q, kt, v, gates, *, B, S, L):
    T = B * S
    nc = S // L
    fwd = lambda b, c: b * nc + c
    bwd = lambda b, c: b * nc + (nc - 1 - c)

    def specs(idx):
        return [
            pl.BlockSpec((L, D_MODEL), lambda b, c: (idx(b, c), 0)),
            pl.BlockSpec((D_MODEL, L), lambda b, c: (0, idx(b, c))),
            pl.BlockSpec((L, D_MODEL), lambda b, c: (idx(b, c), 0)),
            pl.BlockSpec((L, LANES), lambda b, c: (idx(b, c), 0)),
        ]

    return pl.pallas_call(
        functools.partial(_mlstm_body, L=L),
        grid=(B, nc),
        in_specs=specs(fwd) + specs(bwd),
        out_specs=[
            pl.BlockSpec((L, D_MODEL), lambda b, c: (fwd(b, c), 0)),
            pl.BlockSpec((L, D_MODEL), lambda b, c: (bwd(b, c), 0)),
        ],
        out_shape=[jax.ShapeDtypeStruct((T, D_MODEL), F32)] * 2,
        scratch_shapes=[
            pltpu.VMEM((2 * N_HEADS, HEAD_DIM, HEAD_DIM), F32),
            pltpu.VMEM((2 * N_HEADS, HEAD_DIM, LANES), F32),
            pltpu.VMEM((2 * N_HEADS, 1, LANES), F32),
        ],
        compiler_params=_cparams(("parallel", "arbitrary")),
        name="mlstm_scan",
    )(q, kt, v, gates, q, kt, v, gates)


def _lru_dir(x, prev, nxt, is_first, is_last, cw_ref, cb_ref, wr_ref, br_ref, wi_ref, bi_ref,
             lam_ref, h_ref, ext_scr, a_scr, u_scr, carry_scr, *, ts, d, rev):
    xc = _conv_tile(x, prev, nxt, is_first, is_last, cw_ref, cb_ref, ext_scr, ts)
    xcb = xc.astype(BF16)
    lam = lam_ref[d:d + 1, :]
    sp = jnp.maximum(-lam, 0.0) + jnp.log(1.0 + jnp.exp(-jnp.abs(lam)))
    for n in range(LRU_BLOCKS):
        sl = slice(n * LRU_BLOCK_DIM, (n + 1) * LRU_BLOCK_DIM)
        xb = xcb[:, sl]
        r = _sigmoid(jnp.dot(xb, wr_ref[d, n], preferred_element_type=F32) + br_ref[d:d + 1, sl])
        ig = _sigmoid(jnp.dot(xb, wi_ref[d, n], preferred_element_type=F32) + bi_ref[d:d + 1, sl])
        a = jnp.exp(-LRU_C * r * sp[:, sl])
        a_scr[:, sl] = a
        u_scr[:, sl] = jnp.sqrt(1.0 - a * a) * (ig * xc[:, sl])

    sub = lax.broadcasted_iota(jnp.int32, (ts, D_MODEL), 0) % SUBLANES
    A = a_scr[...]
    U = u_scr[...]
    for step in (1, 2, 4):
        if rev:
            a_sh = pltpu.roll(A, ts - step, axis=0)
            u_sh = pltpu.roll(U, ts - step, axis=0)
            ok = sub < SUBLANES - step
        else:
            a_sh = pltpu.roll(A, step, axis=0)
            u_sh = pltpu.roll(U, step, axis=0)
            ok = sub >= step
        U = jnp.where(ok, A * u_sh + U, U)
        A = jnp.where(ok, A * a_sh, A)
    a_scr[...] = A
    u_scr[...] = U

    n_grp = ts // SUBLANES

    def body(k, carry):
        g = (n_grp - 1 - k) if rev else k
        r0 = pl.multiple_of(g * SUBLANES, SUBLANES)
        hg = a_scr[pl.ds(r0, SUBLANES), :] * carry + u_scr[pl.ds(r0, SUBLANES), :]
        h_ref[pl.ds(r0, SUBLANES), :] = hg
        return hg[0:1, :] if rev else hg[SUBLANES - 1:SUBLANES, :]

    carry_scr[d:d + 1, :] = lax.fori_loop(0, n_grp, body, carry_scr[d:d + 1, :])


def _lru_body(xf, pf, nf, xb, pb, nb, cw_ref, cb_ref, wr_ref, br_ref, wi_ref, bi_ref, lam_ref,
              hf_ref, hb_ref, ext_scr, a_scr, u_scr, carry_scr, *, ts, nt):
    i = pl.program_id(1)

    @pl.when(i == 0)
    def _():
        carry_scr[...] = jnp.zeros_like(carry_scr)

    common = (cw_ref, cb_ref, wr_ref, br_ref, wi_ref, bi_ref, lam_ref)
    _lru_dir(xf[...], pf[...], nf[...], i == 0, i == nt - 1, *common, hf_ref,
             ext_scr, a_scr, u_scr, carry_scr, ts=ts, d=0, rev=False)
    _lru_dir(xb[...], pb[...], nb[...], i == nt - 1, i == 0, *common, hb_ref,
             ext_scr, a_scr, u_scr, carry_scr, ts=ts, d=1, rev=True)


def _lru(proj, cw, cb, wr, br, wi, bi, lam, *, B, S, ts):
    T = B * S
    nt = S // ts
    fwd = lambda b, i: b * nt + i
    bwd = lambda b, i: b * nt + (nt - 1 - i)
    mk = _halo_specs(2, ts, B * nt)
    full = lambda shape: pl.BlockSpec(shape, lambda b, i: (0,) * len(shape))
    return pl.pallas_call(
        functools.partial(_lru_body, ts=ts, nt=nt),
        grid=(B, nt),
        in_specs=mk(fwd) + mk(bwd) + [
            full((4, D_MODEL)), full((1, D_MODEL)),
            full((2, LRU_BLOCKS, LRU_BLOCK_DIM, LRU_BLOCK_DIM)), full((2, D_MODEL)),
            full((2, LRU_BLOCKS, LRU_BLOCK_DIM, LRU_BLOCK_DIM)), full((2, D_MODEL)),
            full((2, D_MODEL)),
        ],
        out_specs=[
            pl.BlockSpec((ts, D_MODEL), lambda b, i: (fwd(b, i), 0)),
            pl.BlockSpec((ts, D_MODEL), lambda b, i: (bwd(b, i), 0)),
        ],
        out_shape=[jax.ShapeDtypeStruct((T, D_MODEL), F32)] * 2,
        scratch_shapes=[
            pltpu.VMEM((ts + 2 * HALO, D_MODEL), F32),
            pltpu.VMEM((ts, D_MODEL), F32),
            pltpu.VMEM((ts, D_MODEL), F32),
            pltpu.VMEM((2, D_MODEL), F32),
        ],
        compiler_params=_cparams(("parallel", "arbitrary")),
        name="rglru",
    )(proj, proj, proj, proj, proj, proj, cw, cb, wr, br, wi, bi, lam)


def _merge_body(x_ref, hmf_ref, hmb_ref, opre_ref, hlf_ref, hlb_ref, lgate_ref, gm_ref, gl_ref,
                ng_ref, wbm_ref, wbl_ref, wout_ref, g2_ref, x1_ref, h2t_ref):
    hm = hmf_ref[...] + hmb_ref[...]
    parts = []
    for h in range(N_HEADS):
        sl = slice(h * HEAD_DIM, (h + 1) * HEAD_DIM)
        hh = hm[:, sl]
        ms = jnp.mean(hh * hh, axis=-1, keepdims=True)
        parts.append(hh * lax.rsqrt(ms + EPS) * ng_ref[:, sl])
    hn = jnp.concatenate(parts, axis=-1)
    y_m = (_sigmoid(opre_ref[...]) * hn).astype(BF16)
    y_l = ((hlf_ref[...] + hlb_ref[...]) * _gelu(lgate_ref[...])).astype(BF16)
    merged = (_sigmoid(gm_ref[...]) * jnp.dot(y_m, wbm_ref[...], preferred_element_type=F32)
              + _sigmoid(gl_ref[...]) * jnp.dot(y_l, wbl_ref[...], preferred_element_type=F32))
    x1 = x_ref[...] + jnp.dot(merged.astype(BF16), wout_ref[...], preferred_element_type=F32)
    x1_ref[...] = x1
    ms = jnp.mean(x1 * x1, axis=-1, keepdims=True)
    h2 = x1 * lax.rsqrt(ms + EPS) * g2_ref[...]
    h2t_ref[...] = h2.T.astype(BF16)


def _merge(x2d, hmf, hmb, proj, hlf, hlb, ng, wbm, wbl, wout, g2, *, tm):
    T = x2d.shape[0]
    tile = lambda col: pl.BlockSpec((tm, D_MODEL), lambda i: (i, col))
    vec = pl.BlockSpec((1, D_MODEL), lambda i: (0, 0))
    mat = pl.BlockSpec((D_MODEL, D_MODEL), lambda i: (0, 0))
    return pl.pallas_call(
        _merge_body,
        grid=(T // tm,),
        in_specs=[tile(0), tile(0), tile(0), tile(1), tile(0), tile(0), tile(3), tile(4), tile(5),
                  vec, mat, mat, mat, vec],
        out_specs=[
            pl.BlockSpec((tm, D_MODEL), lambda i: (i, 0)),
            pl.BlockSpec((D_MODEL, tm), lambda i: (0, i)),
        ],
        out_shape=[
            jax.ShapeDtypeStruct((T, D_MODEL), F32),
            jax.ShapeDtypeStruct((D_MODEL, T), BF16),
        ],
        compiler_params=_cparams(("parallel",)),
        name="merge",
    )(x2d, hmf, hmb, proj, hlf, hlb, proj, proj, proj, ng, wbm, wbl, wout, g2)


def _top16_ranked(s):
    n, tb = s.shape
    iota = lax.broadcasted_iota(jnp.int32, (n, tb), 0).astype(F32)
    rank = jnp.full((n, tb), float(PEER_TOPK), F32)
    vals = []
    for j in range(PEER_TOPK):
        m = jnp.max(s, axis=0, keepdims=True)
        idx = jnp.min(jnp.where(s == m, iota, float(n)), axis=0, keepdims=True)
        sel = iota == idx
        rank = jnp.where(sel, float(j), rank)
        s = jnp.where(sel, NEG_INF, s)
        vals.append(m)
    return vals, rank


CODE_BASE = 2.0 ** 100
CODE_STEP = 2.0 ** 96
CODE_TEST = -(2.0 ** 99)


def _extract16_coded(x):
    vals = []
    for j in range(PEER_TOPK):
        m = jnp.max(x, axis=0, keepdims=True)
        x = jnp.where(x == m, -(CODE_BASE + j * CODE_STEP), x)
        vals.append(m)
    return vals, x


def _top16_ranked_notie(s):
    vals, coded = _extract16_coded(s)
    taken = coded < CODE_TEST
    rank = jnp.where(taken, (-coded - CODE_BASE) * (1.0 / CODE_STEP), float(PEER_TOPK))
    count = jnp.sum(jnp.where(taken, 1.0, 0.0), axis=0, keepdims=True)
    return vals, rank, count


_CAND_ROW_GROUPS = [(0, 0, 8), (0, 8, 8), (1, 0, 8), (2, 0, 5), (3, 0, 4), (4, 0, 3),
                    (5, 0, 2), (6, 0, 2), (7, 0, 2)]


def _pair_candidates(v1, v2):
    tb = v1[0].shape[1]
    v1_hi = jnp.concatenate(v1[8:], axis=0)
    v2_lo = jnp.concatenate(v2[:8], axis=0)
    v2_hi = jnp.concatenate(v2[8:], axis=0)
    sub = lax.broadcasted_iota(jnp.int32, (SUBLANES, tb), 0)
    cands, poss, valids = [], [], []
    for j1, base, nv in _CAND_ROW_GROUPS:
        c = v1[j1] + (v2_lo if base == 0 else v2_hi)
        cands.append(jnp.where(sub < nv, c, NEG_INF))
        poss.append((sub + (j1 * PEER_TOPK + base)).astype(F32))
        valids.append(jnp.where(sub < nv, 1.0, 0.0))
    cands.append(v1_hi + v2[0])
    poss.append(((sub + SUBLANES) * PEER_TOPK).astype(F32))
    valids.append(jnp.ones((SUBLANES, tb), F32))
    return (jnp.concatenate(cands, axis=0), jnp.concatenate(poss, axis=0),
            jnp.concatenate(valids, axis=0))


def _staircase(taken, cand, cmax):
    Z = jnp.sum(taken * jnp.exp(cand - cmax), axis=0, keepdims=True)
    grp = lambda g: taken[g * SUBLANES:(g + 1) * SUBLANES, :]
    L = [jnp.sum(grp(0) + grp(1), axis=0, keepdims=True)]
    for g in range(2, 9):
        L.append(jnp.sum(grp(g), axis=0, keepdims=True))
    last = grp(9)
    for k in range(SUBLANES):
        L.append(last[k:k + 1, :])
    return L, Z


def _select_pairs(v1, v2):
    cand0, pos, _ = _pair_candidates(v1, v2)
    cand = cand0
    taken = jnp.zeros_like(cand)
    big = float(PEER_TOPK * PEER_TOPK)
    for _ in range(PEER_TOPK):
        m = jnp.max(cand, axis=0, keepdims=True)
        p = jnp.min(jnp.where(cand == m, pos, big), axis=0, keepdims=True)
        sel = pos == p
        taken = jnp.where(sel, 1.0, taken)
        cand = jnp.where(sel, NEG_INF, cand)
    return _staircase(taken, cand0, v1[0] + v2[0])


def _select_pairs_notie(v1, v2):
    cand0, _, valid = _pair_candidates(v1, v2)
    _, coded = _extract16_coded(cand0)
    taken = jnp.where(coded < CODE_TEST, valid, 0.0)
    L, Z = _staircase(taken, cand0, v1[0] + v2[0])
    return L, Z, jnp.sum(taken, axis=0, keepdims=True)


def _store_selection(h, s1, s2, v1, rank1, v2, rank2, L, Z, r2_scr, e2_scr, lim_scr, coef_scr):
    lim = jnp.full_like(rank1, -0.5)
    for j1 in reversed(range(PEER_TOPK)):
        lim = jnp.where(rank1 < j1 + 0.5, L[j1] - 0.5, lim)
    r2_scr[h] = rank2.astype(BF16)
    e2_scr[h] = jnp.exp(s2 - v2[0]).astype(BF16)
    lim_scr[h] = lim
    coef_scr[h] = jnp.exp(s1 - v1[0]) / Z


MXU_COLS = 256


def _peer_body(h2t_ref, x1_ref, wqt_ref, keys_ref, u0_ref, u_ref, vt_ref, vtl_ref, gf_ref, y_ref,
               r2_scr, e2_scr, lim_scr, coef_scr, limb_scr, coefb_scr, act_scr, wa_scr, acc_scr,
               *, rows):
    j = pl.program_id(1)
    last = pl.num_programs(1) - 1
    tb = h2t_ref.shape[1]
    cur = j % 2
    oth = 1 - cur
    halves = [slice(c * MXU_COLS, (c + 1) * MXU_COLS) for c in range(tb // MXU_COLS)]

    def act_half(u_blk, cols):
        a = jnp.dot(u_blk[...], h2t_ref[:, cols], preferred_element_type=F32)
        return _gelu(a).astype(BF16)

    @pl.when(j == 0)
    def _():
        def select_head(h, carry):
            q_rows = pl.ds(pl.multiple_of(h * (2 * PEER_HALF), 2 * PEER_HALF), 2 * PEER_HALF)
            qt = jnp.dot(wqt_ref[q_rows, :], h2t_ref[...],
                         preferred_element_type=F32).astype(BF16)
            s1 = jnp.dot(keys_ref[2 * h], qt[:PEER_HALF], preferred_element_type=F32)
            s2 = jnp.dot(keys_ref[2 * h + 1], qt[PEER_HALF:], preferred_element_type=F32)
            out = (r2_scr, e2_scr, lim_scr, coef_scr)
            v1, rank1, n1 = _top16_ranked_notie(s1)
            v2, rank2, n2 = _top16_ranked_notie(s2)
            L, Z, n3 = _select_pairs_notie(v1, v2)
            _store_selection(h, s1, s2, v1, rank1, v2, rank2, L, Z, *out)
            k = float(PEER_TOPK)
            miscount = jnp.max(jnp.abs(n1 - k) + jnp.abs(n2 - k) + jnp.abs(n3 - k))

            @pl.when(miscount > 0.0)
            def _():
                v1, rank1 = _top16_ranked(s1)
                v2, rank2 = _top16_ranked(s2)
                L, Z = _select_pairs(v1, v2)
                _store_selection(h, s1, s2, v1, rank1, v2, rank2, L, Z, *out)
            return carry

        lax.fori_loop(0, PEER_HEADS, select_head, 0)
        acc_scr[...] = jnp.zeros_like(acc_scr)
        wa_scr[1] = jnp.zeros(wa_scr.shape[1:], BF16)
        for cols in halves:
            act_scr[0, :, cols] = act_half(u0_ref, cols)

    for h in range(PEER_HEADS):
        for ii in range(rows):
            i1 = j * rows + ii
            lim = jnp.broadcast_to(lim_scr[h, pl.ds(i1, 1), :], (BF16_ROWS, tb))
            coef = jnp.broadcast_to(coef_scr[h, pl.ds(i1, 1), :], (BF16_ROWS, tb))
            limb_scr[h * rows + ii] = lim.astype(BF16)
            coefb_scr[h * rows + ii] = coef.astype(BF16)
    def next_act(c):
        act_scr[oth, :, halves[c]] = act_half(u_ref, halves[c])

    def prev_out(c):
        acc_scr[:, halves[c]] += jnp.dot(vt_ref[...], wa_scr[oth, :, halves[c]],
                                         preferred_element_type=F32)

    n_lt = tb // LANES
    mxu_pieces = ([functools.partial(next_act, c) for c in range(len(halves))]
                  + [functools.partial(prev_out, c) for c in range(len(halves))])
    for lt in range(n_lt):
        ls = slice(lt * LANES, (lt + 1) * LANES)
        for sb in range(PEER_KEYS // BF16_ROWS):
            ss = slice(sb * BF16_ROWS, (sb + 1) * BF16_ROWS)
            accs = [None] * rows
            for h in range(PEER_HEADS):
                r2 = r2_scr[h, ss, ls]
                e2 = e2_scr[h, ss, ls]
                for ii in range(rows):
                    term = jnp.where(r2 < limb_scr[h * rows + ii, :, ls],
                                     e2 * coefb_scr[h * rows + ii, :, ls], 0.0)
                    accs[ii] = term if h == 0 else accs[ii] + term
            for ii in range(rows):
                rs = slice(ii * PEER_KEYS + sb * BF16_ROWS, ii * PEER_KEYS + (sb + 1) * BF16_ROWS)
                wa_scr[cur, rs, ls] = accs[ii] * act_scr[cur, rs, ls]
        for piece in mxu_pieces[lt * len(mxu_pieces) // n_lt:(lt + 1) * len(mxu_pieces) // n_lt]:
            piece()

    @pl.when(j == last)
    def _():
        acc = acc_scr[...] + jnp.dot(vtl_ref[...], wa_scr[cur], preferred_element_type=F32)
        x2 = x1_ref[...] + acc.T
        ms = jnp.mean(x2 * x2, axis=-1, keepdims=True)
        y_ref[...] = x2 * lax.rsqrt(ms + EPS) * gf_ref[...]


def _peer(h2t, x1, wqt, keys, u, vt, gf, *, tb, ec):
    T = x1.shape[0]
    n_chunks = u.shape[0] // ec
    n_q = wqt.shape[0]
    sel = lambda dt: pltpu.VMEM((PEER_HEADS, PEER_KEYS, tb), dt)
    rowb = lambda: pltpu.VMEM((PEER_HEADS * (ec // PEER_KEYS), BF16_ROWS, tb), BF16)
    once = dict(pipeline_mode=pl.Buffered(1))
    return pl.pallas_call(
        functools.partial(_peer_body, rows=ec // PEER_KEYS),
        grid=(T // tb, n_chunks),
        in_specs=[
            pl.BlockSpec((D_MODEL, tb), lambda i, j: (0, i)),
            pl.BlockSpec((tb, D_MODEL), lambda i, j: (i, 0)),
            pl.BlockSpec((n_q, D_MODEL), lambda i, j: (0, 0), **once),
            pl.BlockSpec((2 * PEER_HEADS, PEER_KEYS, PEER_HALF), lambda i, j: (0, 0, 0), **once),
            pl.BlockSpec((ec, D_MODEL), lambda i, j: (0, 0), **once),
            pl.BlockSpec((ec, D_MODEL), lambda i, j: (jnp.minimum(j + 1, n_chunks - 1), 0)),
            pl.BlockSpec((D_MODEL, ec), lambda i, j: (0, jnp.maximum(j - 1, 0))),
            pl.BlockSpec((D_MODEL, ec), lambda i, j: (0, n_chunks - 1), **once),
            pl.BlockSpec((1, D_MODEL), lambda i, j: (0, 0), **once),
        ],
        out_specs=pl.BlockSpec((tb, D_MODEL), lambda i, j: (i, 0)),
        out_shape=jax.ShapeDtypeStruct((T, D_MODEL), F32),
        scratch_shapes=[sel(BF16), sel(BF16), sel(F32), sel(F32), rowb(), rowb(),
                        pltpu.VMEM((2, ec, tb), BF16),
                        pltpu.VMEM((2, ec, tb), BF16),
                        pltpu.VMEM((D_MODEL, tb), F32)],
        compiler_params=_cparams(("parallel", "arbitrary")),
        name="peer",
    )(h2t, x1, wqt, keys, u, u, vt, vt, gf)


def _layer(x, p, *, tm_in, tq, L, ts, tm_merge, tb, ec):
    B, S, D = x.shape
    T = B * S
    x2d = x.reshape(T, D)
    w_in = p["w_in"]
    g0 = 2 * D_MODEL
    w_main = jnp.concatenate([w_in[:, :g0], w_in[:, g0 + N_GATES:]], axis=1).astype(BF16)
    w_gate = jnp.pad(w_in[:, g0:g0 + N_GATES], ((0, 0), (0, LANES - N_GATES))).astype(BF16)
    b_gate = jnp.pad(p["b_gates"], (0, LANES - N_GATES)).reshape(1, LANES)
    row = lambda a: a.reshape(1, -1)

    proj, gates = _in_proj(x2d, row(p["norm1_g"]), w_main, w_gate, b_gate, tm=tm_in)
    q, kt, v = _qkv(proj, p["mlstm_conv_w"], row(p["mlstm_conv_b"]),
                    p["mlstm_w_q"].astype(BF16), p["mlstm_w_k"].astype(BF16),
                    p["mlstm_w_v"].astype(BF16), S=S, tq=tq)
    hmf, hmb = _mlstm_scan(q, kt, v, gates, B=B, S=S, L=L)
    hlf, hlb = _lru(proj, p["lru_conv_w"], row(p["lru_conv_b"]),
                    p["lru_w_r"].astype(BF16), p["lru_b_r"], p["lru_w_i"].astype(BF16),
                    p["lru_b_i"], p["lru_lambda"], B=B, S=S, ts=ts)
    x1, h2t = _merge(x2d, hmf, hmb, proj, hlf, hlb, row(p["mlstm_norm_g"]),
                     p["w_branch_mlstm"].astype(BF16), p["w_branch_lru"].astype(BF16),
                     p["w_out"].astype(BF16), row(p["norm2_g"]), tm=tm_merge)
    n_keys = PEER_HEADS * 2
    keys = p["peer_sub_keys"].reshape(n_keys, PEER_KEYS, PEER_HALF).astype(BF16)
    y = _peer(h2t, x1, p["peer_w_q"].T.astype(BF16), keys, p["peer_u"].astype(BF16),
              p["peer_v"].T.astype(BF16), row(p["final_norm_g"]), tb=tb, ec=ec)
    return y.reshape(B, S, D)


def kernel(x, norm1_g, w_in, b_gates, mlstm_conv_w, mlstm_conv_b, mlstm_w_q, mlstm_w_k, mlstm_w_v,
           mlstm_norm_g, lru_conv_w, lru_conv_b, lru_w_r, lru_b_r, lru_w_i, lru_b_i, lru_lambda,
           w_branch_mlstm, w_branch_lru, w_out, norm2_g, peer_w_q, peer_sub_keys, peer_u, peer_v,
           final_norm_g):
    p = dict(norm1_g=norm1_g[0], w_in=w_in[0], b_gates=b_gates[0], mlstm_conv_w=mlstm_conv_w[0],
             mlstm_conv_b=mlstm_conv_b[0], mlstm_w_q=mlstm_w_q[0], mlstm_w_k=mlstm_w_k[0],
             mlstm_w_v=mlstm_w_v[0], mlstm_norm_g=mlstm_norm_g[0], lru_conv_w=lru_conv_w[0],
             lru_conv_b=lru_conv_b[0], lru_w_r=lru_w_r[0], lru_b_r=lru_b_r[0], lru_w_i=lru_w_i[0],
             lru_b_i=lru_b_i[0], lru_lambda=lru_lambda[0], w_branch_mlstm=w_branch_mlstm[0],
             w_branch_lru=w_branch_lru[0], w_out=w_out[0], norm2_g=norm2_g[0],
             peer_w_q=peer_w_q[0], peer_sub_keys=peer_sub_keys[0], peer_u=peer_u[0],
             peer_v=peer_v[0], final_norm_g=final_norm_g)
    return _layer(x, p, tm_in=1024, tq=512, L=128, ts=512, tm_merge=256, tb=512, ec=1024)
```

```python
import functools
import math

import jax
import jax.numpy as jnp
from jax import lax
from jax.experimental import pallas as pl
from jax.experimental.pallas import tpu as pltpu

F32 = jnp.float32
BF16 = jnp.bfloat16

D_MODEL = 1024
N_HEADS = 4
HEAD_DIM = 256
N_GATES = 16
LRU_BLOCKS = 8
LRU_BLOCK_DIM = 128
LRU_C = 8.0
PEER_HEADS = 8
PEER_KEYS = 128
PEER_HALF = 128
PEER_TOPK = 16
EPS = 1e-6
LANES = 128
SUBLANES = 8
BF16_ROWS = 16
HALO = SUBLANES
NEG_INF = float("-inf")

VMEM_LIMIT = 56 * 1024 * 1024


def _cparams(sem, flags=None):
    return pltpu.CompilerParams(dimension_semantics=sem, vmem_limit_bytes=VMEM_LIMIT, flags=flags)


def _sigmoid(x):
    return 0.5 * jnp.tanh(0.5 * x) + 0.5


def _gelu(x):
    return 0.5 * x * (1.0 + lax.erf(x * (1.0 / math.sqrt(2.0))))


def _log_sigmoid(x):
    return jnp.minimum(x, 0.0) - jnp.log(1.0 + jnp.exp(-jnp.abs(x)))


def _inproj_body(x_ref, g_ref, w_ref, wg_ref, bg_ref, o_ref, gate_ref, h_scr):
    @pl.when(pl.program_id(1) == 0)
    def _():
        x = x_ref[...]
        ms = jnp.mean(x * x, axis=-1, keepdims=True)
        hb = (x * lax.rsqrt(ms + EPS) * g_ref[...]).astype(BF16)
        h_scr[...] = hb
        gate_ref[...] = jnp.dot(hb, wg_ref[...], preferred_element_type=F32) + bg_ref[...]

    o_ref[...] = jnp.dot(h_scr[...], w_ref[...], preferred_element_type=F32)


def _in_proj(x2d, g1, w_main, w_gate, b_gate, *, tm):
    T = x2d.shape[0]
    n_col = w_main.shape[1] // D_MODEL
    return pl.pallas_call(
        _inproj_body,
        grid=(T // tm, n_col),
        in_specs=[
            pl.BlockSpec((tm, D_MODEL), lambda i, j: (i, 0)),
            pl.BlockSpec((1, D_MODEL), lambda i, j: (0, 0)),
            pl.BlockSpec((D_MODEL, D_MODEL), lambda i, j: (0, j)),
            pl.BlockSpec((D_MODEL, LANES), lambda i, j: (0, 0)),
            pl.BlockSpec((1, LANES), lambda i, j: (0, 0)),
        ],
        out_specs=[
            pl.BlockSpec((tm, D_MODEL), lambda i, j: (i, j)),
            pl.BlockSpec((tm, LANES), lambda i, j: (i, 0)),
        ],
        out_shape=[
            jax.ShapeDtypeStruct((T, n_col * D_MODEL), F32),
            jax.ShapeDtypeStruct((T, LANES), F32),
        ],
        scratch_shapes=[pltpu.VMEM((tm, D_MODEL), BF16)],
        compiler_params=_cparams(("parallel", "arbitrary")),
        name="in_proj",
    )(x2d, g1, w_main, w_gate, b_gate)


def _conv_tile(x, prev, nxt, is_first, is_last, cw_ref, cb_ref, ext_scr, tq):
    ext_scr[0:HALO, :] = jnp.where(is_first, 0.0, prev)
    ext_scr[HALO:HALO + tq, :] = x
    ext_scr[HALO + tq:2 * HALO + tq, :] = jnp.where(is_last, 0.0, nxt)
    acc = cb_ref[...] + cw_ref[0:1, :] * ext_scr[HALO - 2:HALO - 2 + tq, :]
    for j in range(1, 4):
        acc = acc + cw_ref[j:j + 1, :] * ext_scr[HALO - 2 + j:HALO - 2 + j + tq, :]
    return acc


def _halo_specs(col, tq, n_tiles):
    r = tq // HALO

    def specs(tile_of):
        return [
            pl.BlockSpec((tq, D_MODEL), lambda *a: (tile_of(*a), col)),
            pl.BlockSpec((HALO, D_MODEL), lambda *a: (jnp.maximum(tile_of(*a) * r - 1, 0), col)),
            pl.BlockSpec((HALO, D_MODEL),
                         lambda *a: (jnp.minimum((tile_of(*a) + 1) * r, n_tiles * r - 1), col)),
        ]

    return specs


def _qkv_body(xm_ref, prev_ref, next_ref, cw_ref, cb_ref, wq_ref, wk_ref, wv_ref,
              q_ref, kt_ref, v_ref, ext_scr, *, tq, tiles_per_seq):
    i = pl.program_id(0)
    pos = i % tiles_per_seq
    xm = xm_ref[...]
    acc = _conv_tile(xm, prev_ref[...], next_ref[...], pos == 0, pos == tiles_per_seq - 1,
                     cw_ref, cb_ref, ext_scr, tq)
    xc = acc * _sigmoid(acc)
    for h in range(N_HEADS):
        sl = slice(h * HEAD_DIM, (h + 1) * HEAD_DIM)
        xch = xc[:, sl].astype(BF16)
        q_ref[:, sl] = jnp.dot(xch, wq_ref[h], preferred_element_type=F32).astype(BF16)
        kh = jnp.dot(xch, wk_ref[h], preferred_element_type=F32) * (1.0 / math.sqrt(HEAD_DIM))
        kt_ref[sl, :] = kh.T.astype(BF16)
        v_ref[:, sl] = jnp.dot(xm[:, sl].astype(BF16), wv_ref[h],
                               preferred_element_type=F32).astype(BF16)


def _qkv(proj, cw, cb, wq, wk, wv, *, S, tq):
    T = proj.shape[0]
    n_tiles = T // tq
    main, prev, nxt = _halo_specs(0, tq, n_tiles)(lambda i: i)
    wspec = pl.BlockSpec((N_HEADS, HEAD_DIM, HEAD_DIM), lambda i: (0, 0, 0))
    return pl.pallas_call(
        functools.partial(_qkv_body, tq=tq, tiles_per_seq=S // tq),
        grid=(n_tiles,),
        in_specs=[main, prev, nxt,
                  pl.BlockSpec((4, D_MODEL), lambda i: (0, 0)),
                  pl.BlockSpec((1, D_MODEL), lambda i: (0, 0)),
                  wspec, wspec, wspec],
        out_specs=[
            pl.BlockSpec((tq, D_MODEL), lambda i: (i, 0)),
            pl.BlockSpec((D_MODEL, tq), lambda i: (0, i)),
            pl.BlockSpec((tq, D_MODEL), lambda i: (i, 0)),
        ],
        out_shape=[
            jax.ShapeDtypeStruct((T, D_MODEL), BF16),
            jax.ShapeDtypeStruct((D_MODEL, T), BF16),
            jax.ShapeDtypeStruct((T, D_MODEL), BF16),
        ],
        scratch_shapes=[pltpu.VMEM((tq + 2 * HALO, D_MODEL), F32)],
        compiler_params=_cparams(("parallel",)),
        name="mlstm_qkv",
    )(proj, proj, proj, cw, cb, wq, wk, wv)


def _mlstm_body(qf, ktf, vf, gf, qb, ktb, vb, gb, hf_ref, hb_ref, c_scr, n_scr, m_scr, *, L):
    @pl.when(pl.program_id(1) == 0)
    def _():
        c_scr[...] = jnp.zeros_like(c_scr)
        n_scr[...] = jnp.zeros_like(n_scr)
        m_scr[...] = jnp.zeros_like(m_scr)

    row = lax.broadcasted_iota(jnp.int32, (L, L), 0)
    col = lax.broadcasted_iota(jnp.int32, (L, L), 1)
    ones = jnp.ones((L, LANES), BF16)
    dirs = ((qf, ktf, vf, gf, hf_ref, False), (qb, ktb, vb, gb, hb_ref, True))
    chains = []
    for q_ref, kt_ref, v_ref, g_ref, h_ref, rev in dirs:
        G = g_ref[...]
        mask = (row <= col) if rev else (row >= col)
        cum = jnp.dot(mask.astype(F32), _log_sigmoid(G), precision=lax.Precision.HIGHEST,
                      preferred_element_type=F32)
        GT = G.T
        cumT = cum.T
        for h in range(N_HEADS):
            icol = (2 * N_HEADS if rev else 0) + h
            fcol = icol + N_HEADS
            b_col = cum[:, fcol:fcol + 1]
            chains.append(dict(
                hd=h + (N_HEADS if rev else 0), sl=slice(h * HEAD_DIM, (h + 1) * HEAD_DIM),
                q_ref=q_ref, kt_ref=kt_ref, v_ref=v_ref, h_ref=h_ref, mask=mask, b_col=b_col,
                r_row=GT[icol:icol + 1, :] - cumT[fcol:fcol + 1, :],
                g=b_col[0:1, :] if rev else b_col[L - 1:L, :]))

    for c in chains:
        m = m_scr[c["hd"]][:, 0:1]
        d_log = jnp.where(c["mask"], c["b_col"] + c["r_row"], NEG_INF)
        inter = c["b_col"] + m
        m_t = jnp.maximum(inter, jnp.max(d_log, axis=-1, keepdims=True))
        w_row = c["g"] + c["r_row"]
        m_new = jnp.maximum(c["g"] + m, jnp.max(w_row, axis=-1, keepdims=True))
        c.update(m=m, d_log=d_log, inter=inter, m_t=m_t, w_row=w_row, m_new=m_new)
    for c in chains:
        c["w_intra"] = jnp.exp(c["d_log"] - c["m_t"])
        c["w_inter"] = jnp.broadcast_to(jnp.exp(c["inter"] - c["m_t"]), (L, LANES))
        c["floor"] = jnp.broadcast_to(jnp.exp(-c["m_t"]), (L, LANES))
        c["decay"] = jnp.exp(c["g"] + c["m"] - c["m_new"])
        c["w_state"] = jnp.exp(c["w_row"] - c["m_new"])
    for c in chains:
        qc = c["q_ref"][:, c["sl"]]
        c["qc"] = qc
        c["s"] = (jnp.dot(qc, c["kt_ref"][c["sl"], :], preferred_element_type=F32)
                  * c["w_intra"]).astype(BF16)
        c["qC"] = jnp.dot(qc, c_scr[c["hd"]].astype(BF16), preferred_element_type=F32)
        c["qn"] = jnp.dot(qc, n_scr[c["hd"]].astype(BF16), preferred_element_type=F32)
    for c in chains:
        vc = c["v_ref"][:, c["sl"]]
        sv = jnp.dot(c["s"], vc, preferred_element_type=F32)
        s_sum = jnp.dot(c["s"], ones, preferred_element_type=F32)
        den = c["w_inter"] * c["qn"] + s_sum
        inv = 1.0 / jnp.maximum(jnp.abs(den), c["floor"])
        two = lambda a: jnp.concatenate([a] * (HEAD_DIM // LANES), axis=-1)
        c["h_ref"][:, c["sl"]] = (two(c["w_inter"]) * c["qC"] + sv) * two(inv)
    for c in chains:
        hd = c["hd"]
        kwt = (c["kt_ref"][c["sl"], :].astype(F32) * c["w_state"]).astype(BF16)
        vc = c["v_ref"][:, c["sl"]]
        c_scr[hd] = c["decay"] * c_scr[hd] + jnp.dot(kwt, vc, preferred_element_type=F32)
        n_scr[hd] = c["decay"] * n_scr[hd] + jnp.dot(kwt, ones, preferred_element_type=F32)
        m_scr[hd] = jnp.broadcast_to(c["m_new"], (1, LANES))


def _mlstm_scan(q, kt, v, gates, *, B, S, L):
    T = B * S
    nc = S // L
    fwd = lambda b, c: b * nc + c
    bwd = lambda b, c: b * nc + (nc - 1 - c)

    def specs(idx):
        return [
            pl.BlockSpec((L, D_MODEL), lambda b, c: (idx(b, c), 0)),
            pl.BlockSpec((D_MODEL, L), lambda b, c: (0, idx(b, c))),
            pl.BlockSpec((L, D_MODEL), lambda b, c: (idx(b, c), 0)),
            pl.BlockSpec((L, LANES), lambda b, c: (idx(b, c), 0)),
        ]

    return pl.pallas_call(
        functools.partial(_mlstm_body, L=L),
        grid=(B, nc),
        in_specs=specs(fwd) + specs(bwd),
        out_specs=[
            pl.BlockSpec((L, D_MODEL), lambda b, c: (fwd(b, c), 0)),
            pl.BlockSpec((L, D_MODEL), lambda b, c: (bwd(b, c), 0)),
        ],
        out_shape=[jax.ShapeDtypeStruct((T, D_MODEL), F32)] * 2,
        scratch_shapes=[
            pltpu.VMEM((2 * N_HEADS, HEAD_DIM, HEAD_DIM), F32),
            pltpu.VMEM((2 * N_HEADS, HEAD_DIM, LANES), F32),
            pltpu.VMEM((2 * N_HEADS, 1, LANES), F32),
        ],
        compiler_params=_cparams(("parallel", "arbitrary")),
        name="mlstm_scan",
    )(q, kt, v, gates, q, kt, v, gates)


def _lru_dir(x, prev, nxt, is_first, is_last, cw_ref, cb_ref, wr_ref, br_ref, wi_ref, bi_ref,
             lam_ref, h_ref, ext_scr, a_scr, u_scr, carry_scr, *, ts, d, rev):
    xc = _conv_tile(x, prev, nxt, is_first, is_last, cw_ref, cb_ref, ext_scr, ts)
    xcb = xc.astype(BF16)
    lam = lam_ref[d:d + 1, :]
    sp = jnp.maximum(-lam, 0.0) + jnp.log(1.0 + jnp.exp(-jnp.abs(lam)))
    for n in range(LRU_BLOCKS):
        sl = slice(n * LRU_BLOCK_DIM, (n + 1) * LRU_BLOCK_DIM)
        xb = xcb[:, sl]
        r = _sigmoid(jnp.dot(xb, wr_ref[d, n], preferred_element_type=F32) + br_ref[d:d + 1, sl])
        ig = _sigmoid(jnp.dot(xb, wi_ref[d, n], preferred_element_type=F32) + bi_ref[d:d + 1, sl])
        a = jnp.exp(-LRU_C * r * sp[:, sl])
        a_scr[:, sl] = a
        u_scr[:, sl] = jnp.sqrt(1.0 - a * a) * (ig * xc[:, sl])

    sub = lax.broadcasted_iota(jnp.int32, (ts, D_MODEL), 0) % SUBLANES
    A = a_scr[...]
    U = u_scr[...]
    for step in (1, 2, 4):
        if rev:
            a_sh = pltpu.roll(A, ts - step, axis=0)
            u_sh = pltpu.roll(U, ts - step, axis=0)
            ok = sub < SUBLANES - step
        else:
            a_sh = pltpu.roll(A, step, axis=0)
            u_sh = pltpu.roll(U, step, axis=0)
            ok = sub >= step
        U = jnp.where(ok, A * u_sh + U, U)
        A = jnp.where(ok, A * a_sh, A)
    a_scr[...] = A
    u_scr[...] = U

    n_grp = ts // SUBLANES

    def body(k, carry):
        g = (n_grp - 1 - k) if rev else k
        r0 = pl.multiple_of(g * SUBLANES, SUBLANES)
        hg = a_scr[pl.ds(r0, SUBLANES), :] * carry + u_scr[pl.ds(r0, SUBLANES), :]
        h_ref[pl.ds(r0, SUBLANES), :] = hg
        return hg[0:1, :] if rev else hg[SUBLANES - 1:SUBLANES, :]

    carry_scr[d:d + 1, :] = lax.fori_loop(0, n_grp, body, carry_scr[d:d + 1, :])


def _lru_body(xf, pf, nf, xb, pb, nb, cw_ref, cb_ref, wr_ref, br_ref, wi_ref, bi_ref, lam_ref,
              hf_ref, hb_ref, ext_scr, a_scr, u_scr, carry_scr, *, ts, nt):
    i = pl.program_id(1)

    @pl.when(i == 0)
    def _():
        carry_scr[...] = jnp.zeros_like(carry_scr)

    common = (cw_ref, cb_ref, wr_ref, br_ref, wi_ref, bi_ref, lam_ref)
    _lru_dir(xf[...], pf[...], nf[...], i == 0, i == nt - 1, *common, hf_ref,
             ext_scr, a_scr, u_scr, carry_scr, ts=ts, d=0, rev=False)
    _lru_dir(xb[...], pb[...], nb[...], i == nt - 1, i == 0, *common, hb_ref,
             ext_scr, a_scr, u_scr, carry_scr, ts=ts, d=1, rev=True)


def _lru(proj, cw, cb, wr, br, wi, bi, lam, *, B, S, ts):
    T = B * S
    nt = S // ts
    fwd = lambda b, i: b * nt + i
    bwd = lambda b, i: b * nt + (nt - 1 - i)
    mk = _halo_specs(2, ts, B * nt)
    full = lambda shape: pl.BlockSpec(shape, lambda b, i: (0,) * len(shape))
    return pl.pallas_call(
        functools.partial(_lru_body, ts=ts, nt=nt),
        grid=(B, nt),
        in_specs=mk(fwd) + mk(bwd) + [
            full((4, D_MODEL)), full((1, D_MODEL)),
            full((2, LRU_BLOCKS, LRU_BLOCK_DIM, LRU_BLOCK_DIM)), full((2, D_MODEL)),
            full((2, LRU_BLOCKS, LRU_BLOCK_DIM, LRU_BLOCK_DIM)), full((2, D_MODEL)),
            full((2, D_MODEL)),
        ],
        out_specs=[
            pl.BlockSpec((ts, D_MODEL), lambda b, i: (fwd(b, i), 0)),
            pl.BlockSpec((ts, D_MODEL), lambda b, i: (bwd(b, i), 0)),
        ],
        out_shape=[jax.ShapeDtypeStruct((T, D_MODEL), F32)] * 2,
        scratch_shapes=[
            pltpu.VMEM((ts + 2 * HALO, D_MODEL), F32),
            pltpu.VMEM((ts, D_MODEL), F32),
            pltpu.VMEM((ts, D_MODEL), F32),
            pltpu.VMEM((2, D_MODEL), F32),
        ],
        compiler_params=_cparams(("parallel", "arbitrary")),
        name="rglru",
    )(proj, proj, proj, proj, proj, proj, cw, cb, wr, br, wi, bi, lam)


def _merge_body(x_ref, hmf_ref, hmb_ref, opre_ref, hlf_ref, hlb_ref, lgate_ref, gm_ref, gl_ref,
                ng_ref, wbm_ref, wbl_ref, wout_ref, g2_ref, x1_ref, h2t_ref):
    hm = hmf_ref[...] + hmb_ref[...]
    parts = []
    for h in range(N_HEADS):
        sl = slice(h * HEAD_DIM, (h + 1) * HEAD_DIM)
        hh = hm[:, sl]
        ms = jnp.mean(hh * hh, axis=-1, keepdims=True)
        parts.append(hh * lax.rsqrt(ms + EPS) * ng_ref[:, sl])
    hn = jnp.concatenate(parts, axis=-1)
    y_m = (_sigmoid(opre_ref[...]) * hn).astype(BF16)
    y_l = ((hlf_ref[...] + hlb_ref[...]) * _gelu(lgate_ref[...])).astype(BF16)
    merged = (_sigmoid(gm_ref[...]) * jnp.dot(y_m, wbm_ref[...], preferred_element_type=F32)
              + _sigmoid(gl_ref[...]) * jnp.dot(y_l, wbl_ref[...], preferred_element_type=F32))
    x1 = x_ref[...] + jnp.dot(merged.astype(BF16), wout_ref[...], preferred_element_type=F32)
    x1_ref[...] = x1
    ms = jnp.mean(x1 * x1, axis=-1, keepdims=True)
    h2 = x1 * lax.rsqrt(ms + EPS) * g2_ref[...]
    h2t_ref[...] = h2.T.astype(BF16)


def _merge(x2d, hmf, hmb, proj, hlf, hlb, ng, wbm, wbl, wout, g2, *, tm):
    T = x2d.shape[0]
    tile = lambda col: pl.BlockSpec((tm, D_MODEL), lambda i: (i, col))
    vec = pl.BlockSpec((1, D_MODEL), lambda i: (0, 0))
    mat = pl.BlockSpec((D_MODEL, D_MODEL), lambda i: (0, 0))
    return pl.pallas_call(
        _merge_body,
        grid=(T // tm,),
        in_specs=[tile(0), tile(0), tile(0), tile(1), tile(0), tile(0), tile(3), tile(4), tile(5),
                  vec, mat, mat, mat, vec],
        out_specs=[
            pl.BlockSpec((tm, D_MODEL), lambda i: (i, 0)),
            pl.BlockSpec((D_MODEL, tm), lambda i: (0, i)),
        ],
        out_shape=[
            jax.ShapeDtypeStruct((T, D_MODEL), F32),
            jax.ShapeDtypeStruct((D_MODEL, T), BF16),
        ],
        compiler_params=_cparams(("parallel",)),
        name="merge",
    )(x2d, hmf, hmb, proj, hlf, hlb, proj, proj, proj, ng, wbm, wbl, wout, g2)


def _top16_ranked(s):
    n, tb = s.shape
    iota = lax.broadcasted_iota(jnp.int32, (n, tb), 0).astype(F32)
    rank = jnp.full((n, tb), float(PEER_TOPK), F32)
    vals = []
    for j in range(PEER_TOPK):
        m = jnp.max(s, axis=0, keepdims=True)
        idx = jnp.min(jnp.where(s == m, iota, float(n)), axis=0, keepdims=True)
        sel = iota == idx
        rank = jnp.where(sel, float(j), rank)
        s = jnp.where(sel, NEG_INF, s)
        vals.append(m)
    return vals, rank


CODE_BASE = 2.0 ** 100
CODE_STEP = 2.0 ** 96
CODE_TEST = -(2.0 ** 99)


def _extract16_coded(x):
    vals = []
    for j in range(PEER_TOPK):
        m = jnp.max(x, axis=0, keepdims=True)
        x = jnp.where(x == m, -(CODE_BASE + j * CODE_STEP), x)
        vals.append(m)
    return vals, x


def _top16_ranked_notie(s):
    vals, coded = _extract16_coded(s)
    taken = coded < CODE_TEST
    rank = jnp.where(taken, (-coded - CODE_BASE) * (1.0 / CODE_STEP), float(PEER_TOPK))
    count = jnp.sum(jnp.where(taken, 1.0, 0.0), axis=0, keepdims=True)
    return vals, rank, count


_CAND_ROW_GROUPS = [(0, 0, 8), (0, 8, 8), (1, 0, 8), (2, 0, 5), (3, 0, 4), (4, 0, 3),
                    (5, 0, 2), (6, 0, 2), (7, 0, 2)]


def _pair_candidates(v1, v2):
    tb = v1[0].shape[1]
    v1_hi = jnp.concatenate(v1[8:], axis=0)
    v2_lo = jnp.concatenate(v2[:8], axis=0)
    v2_hi = jnp.concatenate(v2[8:], axis=0)
    sub = lax.broadcasted_iota(jnp.int32, (SUBLANES, tb), 0)
    cands, poss, valids = [], [], []
    for j1, base, nv in _CAND_ROW_GROUPS:
        c = v1[j1] + (v2_lo if base == 0 else v2_hi)
        cands.append(jnp.where(sub < nv, c, NEG_INF))
        poss.append((sub + (j1 * PEER_TOPK + base)).astype(F32))
        valids.append(jnp.where(sub < nv, 1.0, 0.0))
    cands.append(v1_hi + v2[0])
    poss.append(((sub + SUBLANES) * PEER_TOPK).astype(F32))
    valids.append(jnp.ones((SUBLANES, tb), F32))
    return (jnp.concatenate(cands, axis=0), jnp.concatenate(poss, axis=0),
            jnp.concatenate(valids, axis=0))


def _staircase(taken, cand, cmax):
    Z = jnp.sum(taken * jnp.exp(cand - cmax), axis=0, keepdims=True)
    grp = lambda g: taken[g * SUBLANES:(g + 1) * SUBLANES, :]
    L = [jnp.sum(grp(0) + grp(1), axis=0, keepdims=True)]
    for g in range(2, 9):
        L.append(jnp.sum(grp(g), axis=0, keepdims=True))
    last = grp(9)
    for k in range(SUBLANES):
        L.append(last[k:k + 1, :])
    return L, Z


def _select_pairs(v1, v2):
    cand0, pos, _ = _pair_candidates(v1, v2)
    cand = cand0
    taken = jnp.zeros_like(cand)
    big = float(PEER_TOPK * PEER_TOPK)
    for _ in range(PEER_TOPK):
        m = jnp.max(cand, axis=0, keepdims=True)
        p = jnp.min(jnp.where(cand == m, pos, big), axis=0, keepdims=True)
        sel = pos == p
        taken = jnp.where(sel, 1.0, taken)
        cand = jnp.where(sel, NEG_INF, cand)
    return _staircase(taken, cand0, v1[0] + v2[0])


def _select_pairs_notie(v1, v2):
    cand0, _, valid = _pair_candidates(v1, v2)
    _, coded = _extract16_coded(cand0)
    taken = jnp.where(coded < CODE_TEST, valid, 0.0)
    L, Z = _staircase(taken, cand0, v1[0] + v2[0])
    return L, Z, jnp.sum(taken, axis=0, keepdims=True)


def _store_selection(h, s1, s2, v1, rank1, v2, rank2, L, Z, r2_scr, e2_scr, lim_scr, coef_scr):
    lim = jnp.full_like(rank1, -0.5)
    for j1 in reversed(range(PEER_TOPK)):
        lim = jnp.where(rank1 < j1 + 0.5, L[j1] - 0.5, lim)
    r2_scr[h] = rank2.astype(BF16)
    e2_scr[h] = jnp.exp(s2 - v2[0]).astype(BF16)
    lim_scr[h] = lim
    coef_scr[h] = jnp.exp(s1 - v1[0]) / Z


MXU_COLS = 256


def _peer_body(h2t_ref, x1_ref, wqt_ref, keys_ref, u0_ref, u_ref, vt_ref, vtl_ref, gf_ref, y_ref,
               r2_scr, e2_scr, lim_scr, coef_scr, limb_scr, coefb_scr, act_scr, wa_scr, acc_scr,
               *, rows):
    j = pl.program_id(1)
    last = pl.num_programs(1) - 1
    tb = h2t_ref.shape[1]
    cur = j % 2
    oth = 1 - cur
    halves = [slice(c * MXU_COLS, (c + 1) * MXU_COLS) for c in range(tb // MXU_COLS)]

    def act_half(u_blk, cols):
        a = jnp.dot(u_blk[...], h2t_ref[:, cols], preferred_element_type=F32)
        return _gelu(a).astype(BF16)

    @pl.when(j == 0)
    def _():
        def select_head(h, carry):
            q_rows = pl.ds(pl.multiple_of(h * (2 * PEER_HALF), 2 * PEER_HALF), 2 * PEER_HALF)
            qt = jnp.dot(wqt_ref[q_rows, :], h2t_ref[...],
                         preferred_element_type=F32).astype(BF16)
            s1 = jnp.dot(keys_ref[2 * h], qt[:PEER_HALF], preferred_element_type=F32)
            s2 = jnp.dot(keys_ref[2 * h + 1], qt[PEER_HALF:], preferred_element_type=F32)
            out = (r2_scr, e2_scr, lim_scr, coef_scr)
            v1, rank1, n1 = _top16_ranked_notie(s1)
            v2, rank2, n2 = _top16_ranked_notie(s2)
            L, Z, n3 = _select_pairs_notie(v1, v2)
            _store_selection(h, s1, s2, v1, rank1, v2, rank2, L, Z, *out)
            k = float(PEER_TOPK)
            miscount = jnp.max(jnp.abs(n1 - k) + jnp.abs(n2 - k) + jnp.abs(n3 - k))

            @pl.when(miscount > 0.0)
            def _():
                v1, rank1 = _top16_ranked(s1)
                v2, rank2 = _top16_ranked(s2)
                L, Z = _select_pairs(v1, v2)
                _store_selection(h, s1, s2, v1, rank1, v2, rank2, L, Z, *out)
            return carry

        lax.fori_loop(0, PEER_HEADS, select_head, 0)
        acc_scr[...] = jnp.zeros_like(acc_scr)
        wa_scr[1] = jnp.zeros(wa_scr.shape[1:], BF16)
        for cols in halves:
            act_scr[0, :, cols] = act_half(u0_ref, cols)

    for h in range(PEER_HEADS):
        for ii in range(rows):
            i1 = j * rows + ii
            lim = jnp.broadcast_to(lim_scr[h, pl.ds(i1, 1), :], (BF16_ROWS, tb))
            coef = jnp.broadcast_to(coef_scr[h, pl.ds(i1, 1), :], (BF16_ROWS, tb))
            limb_scr[h * rows + ii] = lim.astype(BF16)
            coefb_scr[h * rows + ii] = coef.astype(BF16)
    def next_act(c):
        act_scr[oth, :, halves[c]] = act_half(u_ref, halves[c])

    def prev_out(c):
        acc_scr[:, halves[c]] += jnp.dot(vt_ref[...], wa_scr[oth, :, halves[c]],
                                         preferred_element_type=F32)

    n_lt = tb // LANES
    mxu_pieces = ([functools.partial(next_act, c) for c in range(len(halves))]
                  + [functools.partial(prev_out, c) for c in range(len(halves))])
    for lt in range(n_lt):
        ls = slice(lt * LANES, (lt + 1) * LANES)
        for sb in range(PEER_KEYS // BF16_ROWS):
            ss = slice(sb * BF16_ROWS, (sb + 1) * BF16_ROWS)
            accs = [None] * rows
            for h in range(PEER_HEADS):
                r2 = r2_scr[h, ss, ls]
                e2 = e2_scr[h, ss, ls]
                for ii in range(rows):
                    term = jnp.where(r2 < limb_scr[h * rows + ii, :, ls],
                                     e2 * coefb_scr[h * rows + ii, :, ls], 0.0)
                    accs[ii] = term if h == 0 else accs[ii] + term
            for ii in range(rows):
                rs = slice(ii * PEER_KEYS + sb * BF16_ROWS, ii * PEER_KEYS + (sb + 1) * BF16_ROWS)
                wa_scr[cur, rs, ls] = accs[ii] * act_scr[cur, rs, ls]
        for piece in mxu_pieces[lt * len(mxu_pieces) // n_lt:(lt + 1) * len(mxu_pieces) // n_lt]:
            piece()

    @pl.when(j == last)
    def _():
        acc = acc_scr[...] + jnp.dot(vtl_ref[...], wa_scr[cur], preferred_element_type=F32)
        x2 = x1_ref[...] + acc.T
        ms = jnp.mean(x2 * x2, axis=-1, keepdims=True)
        y_ref[...] = x2 * lax.rsqrt(ms + EPS) * gf_ref[...]


def _peer(h2t, x1, wqt, keys, u, vt, gf, *, tb, ec):
    T = x1.shape[0]
    n_chunks = u.shape[0] // ec
    n_q = wqt.shape[0]
    sel = lambda dt: pltpu.VMEM((PEER_HEADS, PEER_KEYS, tb), dt)
    rowb = lambda: pltpu.VMEM((PEER_HEADS * (ec // PEER_KEYS), BF16_ROWS, tb), BF16)
    once = dict(pipeline_mode=pl.Buffered(1))
    return pl.pallas_call(
        functools.partial(_peer_body, rows=ec // PEER_KEYS),
        grid=(T // tb, n_chunks),
        in_specs=[
            pl.BlockSpec((D_MODEL, tb), lambda i, j: (0, i)),
            pl.BlockSpec((tb, D_MODEL), lambda i, j: (i, 0)),
            pl.BlockSpec((n_q, D_MODEL), lambda i, j: (0, 0), **once),
            pl.BlockSpec((2 * PEER_HEADS, PEER_KEYS, PEER_HALF), lambda i, j: (0, 0, 0), **once),
            pl.BlockSpec((ec, D_MODEL), lambda i, j: (0, 0), **once),
            pl.BlockSpec((ec, D_MODEL), lambda i, j: (jnp.minimum(j + 1, n_chunks - 1), 0)),
            pl.BlockSpec((D_MODEL, ec), lambda i, j: (0, jnp.maximum(j - 1, 0))),
            pl.BlockSpec((D_MODEL, ec), lambda i, j: (0, n_chunks - 1), **once),
            pl.BlockSpec((1, D_MODEL), lambda i, j: (0, 0), **once),
        ],
        out_specs=pl.BlockSpec((tb, D_MODEL), lambda i, j: (i, 0)),
        out_shape=jax.ShapeDtypeStruct((T, D_MODEL), F32),
        scratch_shapes=[sel(BF16), sel(BF16), sel(F32), sel(F32), rowb(), rowb(),
                        pltpu.VMEM((2, ec, tb), BF16),
                        pltpu.VMEM((2, ec, tb), BF16),
                        pltpu.VMEM((D_MODEL, tb), F32)],
        compiler_params=_cparams(("parallel", "arbitrary")),
        name="peer",
    )(h2t, x1, wqt, keys, u, u, vt, vt, gf)


def _layer(x, p, *, tm_in, tq, L, ts, tm_merge, tb, ec):
    B, S, D = x.shape
    T = B * S
    x2d = x.reshape(T, D)
    w_in = p["w_in"]
    g0 = 2 * D_MODEL
    w_main = jnp.concatenate([w_in[:, :g0], w_in[:, g0 + N_GATES:]], axis=1).astype(BF16)
    w_gate = jnp.pad(w_in[:, g0:g0 + N_GATES], ((0, 0), (0, LANES - N_GATES))).astype(BF16)
    b_gate = jnp.pad(p["b_gates"], (0, LANES - N_GATES)).reshape(1, LANES)
    row = lambda a: a.reshape(1, -1)

    proj, gates = _in_proj(x2d, row(p["norm1_g"]), w_main, w_gate, b_gate, tm=tm_in)
    q, kt, v = _qkv(proj, p["mlstm_conv_w"], row(p["mlstm_conv_b"]),
                    p["mlstm_w_q"].astype(BF16), p["mlstm_w_k"].astype(BF16),
                    p["mlstm_w_v"].astype(BF16), S=S, tq=tq)
    hmf, hmb = _mlstm_scan(q, kt, v, gates, B=B, S=S, L=L)
    hlf, hlb = _lru(proj, p["lru_conv_w"], row(p["lru_conv_b"]),
                    p["lru_w_r"].astype(BF16), p["lru_b_r"], p["lru_w_i"].astype(BF16),
                    p["lru_b_i"], p["lru_lambda"], B=B, S=S, ts=ts)
    x1, h2t = _merge(x2d, hmf, hmb, proj, hlf, hlb, row(p["mlstm_norm_g"]),
                     p["w_branch_mlstm"].astype(BF16), p["w_branch_lru"].astype(BF16),
                     p["w_out"].astype(BF16), row(p["norm2_g"]), tm=tm_merge)
    n_keys = PEER_HEADS * 2
    keys = p["peer_sub_keys"].reshape(n_keys, PEER_KEYS, PEER_HALF).astype(BF16)
    y = _peer(h2t, x1, p["peer_w_q"].T.astype(BF16), keys, p["peer_u"].astype(BF16),
              p["peer_v"].T.astype(BF16), row(p["final_norm_g"]), tb=tb, ec=ec)
    return y.reshape(B, S, D)


def kernel(x, norm1_g, w_in, b_gates, mlstm_conv_w, mlstm_conv_b, mlstm_w_q, mlstm_w_k, mlstm_w_v,
           mlstm_norm_g, lru_conv_w, lru_conv_b, lru_w_r, lru_b_r, lru_w_i, lru_b_i, lru_lambda,
           w_branch_mlstm, w_branch_lru, w_out, norm2_g, peer_w_q, peer_sub_keys, peer_u, peer_v,
           final_norm_g):
    p = dict(norm1_g=norm1_g[0], w_in=w_in[0], b_gates=b_gates[0], mlstm_conv_w=mlstm_conv_w[0],
             mlstm_conv_b=mlstm_conv_b[0], mlstm_w_q=mlstm_w_q[0], mlstm_w_k=mlstm_w_k[0],
             mlstm_w_v=mlstm_w_v[0], mlstm_norm_g=mlstm_norm_g[0], lru_conv_w=lru_conv_w[0],
             lru_conv_b=lru_conv_b[0], lru_w_r=lru_w_r[0], lru_b_r=lru_b_r[0], lru_w_i=lru_w_i[0],
             lru_b_i=lru_b_i[0], lru_lambda=lru_lambda[0], w_branch_mlstm=w_branch_mlstm[0],
             w_branch_lru=w_branch_lru[0], w_out=w_out[0], norm2_g=norm2_g[0],
             peer_w_q=peer_w_q[0], peer_sub_keys=peer_sub_keys[0], peer_u=peer_u[0],
             peer_v=peer_v[0], final_norm_g=final_norm_g)
    return _layer(x, p, tm_in=1024, tq=512, L=256, ts=512, tm_merge=256, tb=512, ec=1024)
```

```python
import functools
import math

import jax
import jax.numpy as jnp
from jax import lax
from jax.experimental import pallas as pl
from jax.experimental.pallas import tpu as pltpu

F32 = jnp.float32
BF16 = jnp.bfloat16

D_MODEL = 1024
N_HEADS = 4
HEAD_DIM = 256
N_GATES = 16
LRU_BLOCKS = 8
LRU_BLOCK_DIM = 128
LRU_C = 8.0
PEER_HEADS = 8
PEER_KEYS = 128
PEER_HALF = 128
PEER_TOPK = 16
EPS = 1e-6
LANES = 128
SUBLANES = 8
BF16_ROWS = 16
HALO = SUBLANES
NEG_INF = float("-inf")

VMEM_LIMIT = 56 * 1024 * 1024


def _cparams(sem, flags=None):
    return pltpu.CompilerParams(dimension_semantics=sem, vmem_limit_bytes=VMEM_LIMIT, flags=flags)


def _sigmoid(x):
    return 0.5 * jnp.tanh(0.5 * x) + 0.5


def _gelu(x):
    return 0.5 * x * (1.0 + lax.erf(x * (1.0 / math.sqrt(2.0))))


def _log_sigmoid(x):
    return jnp.minimum(x, 0.0) - jnp.log(1.0 + jnp.exp(-jnp.abs(x)))


def _inproj_body(x_ref, g_ref, w_ref, wg_ref, bg_ref, o_ref, gate_ref, h_scr):
    @pl.when(pl.program_id(1) == 0)
    def _():
        x = x_ref[...]
        ms = jnp.mean(x * x, axis=-1, keepdims=True)
        hb = (x * lax.rsqrt(ms + EPS) * g_ref[...]).astype(BF16)
        h_scr[...] = hb
        gate_ref[...] = jnp.dot(hb, wg_ref[...], preferred_element_type=F32) + bg_ref[...]

    o_ref[...] = jnp.dot(h_scr[...], w_ref[...], preferred_element_type=F32)


def _in_proj(x2d, g1, w_main, w_gate, b_gate, *, tm):
    T = x2d.shape[0]
    n_col = w_main.shape[1] // D_MODEL
    return pl.pallas_call(
        _inproj_body,
        grid=(T // tm, n_col),
        in_specs=[
            pl.BlockSpec((tm, D_MODEL), lambda i, j: (i, 0)),
            pl.BlockSpec((1, D_MODEL), lambda i, j: (0, 0)),
            pl.BlockSpec((D_MODEL, D_MODEL), lambda i, j: (0, j)),
            pl.BlockSpec((D_MODEL, LANES), lambda i, j: (0, 0)),
            pl.BlockSpec((1, LANES), lambda i, j: (0, 0)),
        ],
        out_specs=[
            pl.BlockSpec((tm, D_MODEL), lambda i, j: (i, j)),
            pl.BlockSpec((tm, LANES), lambda i, j: (i, 0)),
        ],
        out_shape=[
            jax.ShapeDtypeStruct((T, n_col * D_MODEL), F32),
            jax.ShapeDtypeStruct((T, LANES), F32),
        ],
        scratch_shapes=[pltpu.VMEM((tm, D_MODEL), BF16)],
        compiler_params=_cparams(("parallel", "arbitrary")),
        name="in_proj",
    )(x2d, g1, w_main, w_gate, b_gate)


def _conv_tile(x, prev, nxt, is_first, is_last, cw_ref, cb_ref, ext_scr, tq):
    ext_scr[0:HALO, :] = jnp.where(is_first, 0.0, prev)
    ext_scr[HALO:HALO + tq, :] = x
    ext_scr[HALO + tq:2 * HALO + tq, :] = jnp.where(is_last, 0.0, nxt)
    acc = cb_ref[...] + cw_ref[0:1, :] * ext_scr[HALO - 2:HALO - 2 + tq, :]
    for j in range(1, 4):
        acc = acc + cw_ref[j:j + 1, :] * ext_scr[HALO - 2 + j:HALO - 2 + j + tq, :]
    return acc


def _halo_specs(col, tq, n_tiles):
    r = tq // HALO

    def specs(tile_of):
        return [
            pl.BlockSpec((tq, D_MODEL), lambda *a: (tile_of(*a), col)),
            pl.BlockSpec((HALO, D_MODEL), lambda *a: (jnp.maximum(tile_of(*a) * r - 1, 0), col)),
            pl.BlockSpec((HALO, D_MODEL),
                         lambda *a: (jnp.minimum((tile_of(*a) + 1) * r, n_tiles * r - 1), col)),
        ]

    return specs


def _qkv_body(xm_ref, prev_ref, next_ref, cw_ref, cb_ref, wq_ref, wk_ref, wv_ref,
              q_ref, kt_ref, v_ref, ext_scr, *, tq, tiles_per_seq):
    i = pl.program_id(0)
    pos = i % tiles_per_seq
    xm = xm_ref[...]
    acc = _conv_tile(xm, prev_ref[...], next_ref[...], pos == 0, pos == tiles_per_seq - 1,
                     cw_ref, cb_ref, ext_scr, tq)
    xc = acc * _sigmoid(acc)
    for h in range(N_HEADS):
        sl = slice(h * HEAD_DIM, (h + 1) * HEAD_DIM)
        xch = xc[:, sl].astype(BF16)
        q_ref[:, sl] = jnp.dot(xch, wq_ref[h], preferred_element_type=F32).astype(BF16)
        kh = jnp.dot(xch, wk_ref[h], preferred_element_type=F32) * (1.0 / math.sqrt(HEAD_DIM))
        kt_ref[sl, :] = kh.T.astype(BF16)
        v_ref[:, sl] = jnp.dot(xm[:, sl].astype(BF16), wv_ref[h],
                               preferred_element_type=F32).astype(BF16)


def _qkv(proj, cw, cb, wq, wk, wv, *, S, tq):
    T = proj.shape[0]
    n_tiles = T // tq
    main, prev, nxt = _halo_specs(0, tq, n_tiles)(lambda i: i)
    wspec = pl.BlockSpec((N_HEADS, HEAD_DIM, HEAD_DIM), lambda i: (0, 0, 0))
    return pl.pallas_call(
        functools.partial(_qkv_body, tq=tq, tiles_per_seq=S // tq),
        grid=(n_tiles,),
        in_specs=[main, prev, nxt,
                  pl.BlockSpec((4, D_MODEL), lambda i: (0, 0)),
                  pl.BlockSpec((1, D_MODEL), lambda i: (0, 0)),
                  wspec, wspec, wspec],
        out_specs=[
            pl.BlockSpec((tq, D_MODEL), lambda i: (i, 0)),
            pl.BlockSpec((D_MODEL, tq), lambda i: (0, i)),
            pl.BlockSpec((tq, D_MODEL), lambda i: (i, 0)),
        ],
        out_shape=[
            jax.ShapeDtypeStruct((T, D_MODEL), BF16),
            jax.ShapeDtypeStruct((D_MODEL, T), BF16),
            jax.ShapeDtypeStruct((T, D_MODEL), BF16),
        ],
        scratch_shapes=[pltpu.VMEM((tq + 2 * HALO, D_MODEL), F32)],
        compiler_params=_cparams(("parallel",)),
        name="mlstm_qkv",
    )(proj, proj, proj, cw, cb, wq, wk, wv)


def _mlstm_body(qf, ktf, vf, gf, qb, ktb, vb, gb, hf_ref, hb_ref, c_scr, n_scr, m_scr, *, L):
    @pl.when(pl.program_id(1) == 0)
    def _():
        c_scr[...] = jnp.zeros_like(c_scr)
        n_scr[...] = jnp.zeros_like(n_scr)
        m_scr[...] = jnp.zeros_like(m_scr)

    row = lax.broadcasted_iota(jnp.int32, (L, L), 0)
    col = lax.broadcasted_iota(jnp.int32, (L, L), 1)
    ones = jnp.ones((L, LANES), BF16)
    dirs = ((qf, ktf, vf, gf, hf_ref, False), (qb, ktb, vb, gb, hb_ref, True))
    chains = []
    for q_ref, kt_ref, v_ref, g_ref, h_ref, rev in dirs:
        G = g_ref[...]
        mask = (row <= col) if rev else (row >= col)
        cum = jnp.dot(mask.astype(F32), _log_sigmoid(G), precision=lax.Precision.HIGHEST,
                      preferred_element_type=F32)
        GT = G.T
        cumT = cum.T
        for h in range(N_HEADS):
            icol = (2 * N_HEADS if rev else 0) + h
            fcol = icol + N_HEADS
            b_col = cum[:, fcol:fcol + 1]
            chains.append(dict(
                hd=h + (N_HEADS if rev else 0), sl=slice(h * HEAD_DIM, (h + 1) * HEAD_DIM),
                q_ref=q_ref, kt_ref=kt_ref, v_ref=v_ref, h_ref=h_ref, mask=mask, b_col=b_col,
                r_row=GT[icol:icol + 1, :] - cumT[fcol:fcol + 1, :],
                g=b_col[0:1, :] if rev else b_col[L - 1:L, :]))

    for c in chains:
        m = m_scr[c["hd"]][:, 0:1]
        d_log = jnp.where(c["mask"], c["b_col"] + c["r_row"], NEG_INF)
        inter = c["b_col"] + m
        m_t = jnp.maximum(inter, jnp.max(d_log, axis=-1, keepdims=True))
        w_row = c["g"] + c["r_row"]
        m_new = jnp.maximum(c["g"] + m, jnp.max(w_row, axis=-1, keepdims=True))
        c.update(m=m, d_log=d_log, inter=inter, m_t=m_t, w_row=w_row, m_new=m_new)
    for c in chains:
        c["w_intra"] = jnp.exp(c["d_log"] - c["m_t"])
        c["w_inter"] = jnp.broadcast_to(jnp.exp(c["inter"] - c["m_t"]), (L, LANES))
        c["floor"] = jnp.broadcast_to(jnp.exp(-c["m_t"]), (L, LANES))
        c["decay"] = jnp.exp(c["g"] + c["m"] - c["m_new"])
        c["w_state"] = jnp.exp(c["w_row"] - c["m_new"])
    for c in chains:
        qc = c["q_ref"][:, c["sl"]]
        c["qc"] = qc
        c["s"] = (jnp.dot(qc, c["kt_ref"][c["sl"], :], preferred_element_type=F32)
                  * c["w_intra"]).astype(BF16)
        c["qC"] = jnp.dot(qc, c_scr[c["hd"]].astype(BF16), preferred_element_type=F32)
        c["qn"] = jnp.dot(qc, n_scr[c["hd"]].astype(BF16), preferred_element_type=F32)
    for c in chains:
        vc = c["v_ref"][:, c["sl"]]
        sv = jnp.dot(c["s"], vc, preferred_element_type=F32)
        s_sum = jnp.dot(c["s"], ones, preferred_element_type=F32)
        den = c["w_inter"] * c["qn"] + s_sum
        inv = 1.0 / jnp.maximum(jnp.abs(den), c["floor"])
        two = lambda a: jnp.concatenate([a] * (HEAD_DIM // LANES), axis=-1)
        c["h_ref"][:, c["sl"]] = (two(c["w_inter"]) * c["qC"] + sv) * two(inv)
    for c in chains:
        hd = c["hd"]
        kwt = (c["kt_ref"][c["sl"], :].astype(F32) * c["w_state"]).astype(BF16)
        vc = c["v_ref"][:, c["sl"]]
        c_scr[hd] = c["decay"] * c_scr[hd] + jnp.dot(kwt, vc, preferred_element_type=F32)
        n_scr[hd] = c["decay"] * n_scr[hd] + jnp.dot(kwt, ones, preferred_element_type=F32)
        m_scr[hd] = jnp.broadcast_to(c["m_new"], (1, LANES))


def _mlstm_scan(q, kt, v, gates, *, B, S, L):
    T = B * S
    nc = S // L
    fwd = lambda b, c: b * nc + c
    bwd = lambda b, c: b * nc + (nc - 1 - c)

    def specs(idx):
        return [
            pl.BlockSpec((L, D_MODEL), lambda b, c: (idx(b, c), 0)),
            pl.BlockSpec((D_MODEL, L), lambda b, c: (0, idx(b, c))),
            pl.BlockSpec((L, D_MODEL), lambda b, c: (idx(b, c), 0)),
            pl.BlockSpec((L, LANES), lambda b, c: (idx(b, c), 0)),
        ]

    return pl.pallas_call(
        functools.partial(_mlstm_body, L=L),
        grid=(B, nc),
        in_specs=specs(fwd) + specs(bwd),
        out_specs=[
            pl.BlockSpec((L, D_MODEL), lambda b, c: (fwd(b, c), 0)),
            pl.BlockSpec((L, D_MODEL), lambda b, c: (bwd(b, c), 0)),
        ],
        out_shape=[jax.ShapeDtypeStruct((T, D_MODEL), F32)] * 2,
        scratch_shapes=[
            pltpu.VMEM((2 * N_HEADS, HEAD_DIM, HEAD_DIM), F32),
            pltpu.VMEM((2 * N_HEADS, HEAD_DIM, LANES), F32),
            pltpu.VMEM((2 * N_HEADS, 1, LANES), F32),
        ],
        compiler_params=_cparams(("parallel", "arbitrary")),
        name="mlstm_scan",
    )(q, kt, v, gates, q, kt, v, gates)


def _lru_dir(x, prev, nxt, is_first, is_last, cw_ref, cb_ref, wr_ref, br_ref, wi_ref, bi_ref,
             lam_ref, h_ref, ext_scr, a_scr, u_scr, carry_scr, *, ts, d, rev):
    xc = _conv_tile(x, prev, nxt, is_first, is_last, cw_ref, cb_ref, ext_scr, ts)
    xcb = xc.astype(BF16)
    lam = lam_ref[d:d + 1, :]
    sp = jnp.maximum(-lam, 0.0) + jnp.log(1.0 + jnp.exp(-jnp.abs(lam)))
    for n in range(LRU_BLOCKS):
        sl = slice(n * LRU_BLOCK_DIM, (n + 1) * LRU_BLOCK_DIM)
        xb = xcb[:, sl]
        r = _sigmoid(jnp.dot(xb, wr_ref[d, n], preferred_element_type=F32) + br_ref[d:d + 1, sl])
        ig = _sigmoid(jnp.dot(xb, wi_ref[d, n], preferred_element_type=F32) + bi_ref[d:d + 1, sl])
        a = jnp.exp(-LRU_C * r * sp[:, sl])
        a_scr[:, sl] = a
        u_scr[:, sl] = jnp.sqrt(1.0 - a * a) * (ig * xc[:, sl])

    sub = lax.broadcasted_iota(jnp.int32, (ts, D_MODEL), 0) % SUBLANES
    A = a_scr[...]
    U = u_scr[...]
    for step in (1, 2, 4):
        if rev:
            a_sh = pltpu.roll(A, ts - step, axis=0)
            u_sh = pltpu.roll(U, ts - step, axis=0)
            ok = sub < SUBLANES - step
        else:
            a_sh = pltpu.roll(A, step, axis=0)
            u_sh = pltpu.roll(U, step, axis=0)
            ok = sub >= step
        U = jnp.where(ok, A * u_sh + U, U)
        A = jnp.where(ok, A * a_sh, A)
    a_scr[...] = A
    u_scr[...] = U

    n_grp = ts // SUBLANES

    def body(k, carry):
        g = (n_grp - 1 - k) if rev else k
        r0 = pl.multiple_of(g * SUBLANES, SUBLANES)
        hg = a_scr[pl.ds(r0, SUBLANES), :] * carry + u_scr[pl.ds(r0, SUBLANES), :]
        h_ref[pl.ds(r0, SUBLANES), :] = hg
        return hg[0:1, :] if rev else hg[SUBLANES - 1:SUBLANES, :]

    carry_scr[d:d + 1, :] = lax.fori_loop(0, n_grp, body, carry_scr[d:d + 1, :])


def _lru_body(xf, pf, nf, xb, pb, nb, cw_ref, cb_ref, wr_ref, br_ref, wi_ref, bi_ref, lam_ref,
              hf_ref, hb_ref, ext_scr, a_scr, u_scr, carry_scr, *, ts, nt):
    i = pl.program_id(1)

    @pl.when(i == 0)
    def _():
        carry_scr[...] = jnp.zeros_like(carry_scr)

    common = (cw_ref, cb_ref, wr_ref, br_ref, wi_ref, bi_ref, lam_ref)
    _lru_dir(xf[...], pf[...], nf[...], i == 0, i == nt - 1, *common, hf_ref,
             ext_scr, a_scr, u_scr, carry_scr, ts=ts, d=0, rev=False)
    _lru_dir(xb[...], pb[...], nb[...], i == nt - 1, i == 0, *common, hb_ref,
             ext_scr, a_scr, u_scr, carry_scr, ts=ts, d=1, rev=True)


def _lru(proj, cw, cb, wr, br, wi, bi, lam, *, B, S, ts):
    T = B * S
    nt = S // ts
    fwd = lambda b, i: b * nt + i
    bwd = lambda b, i: b * nt + (nt - 1 - i)
    mk = _halo_specs(2, ts, B * nt)
    full = lambda shape: pl.BlockSpec(shape, lambda b, i: (0,) * len(shape))
    return pl.pallas_call(
        functools.partial(_lru_body, ts=ts, nt=nt),
        grid=(B, nt),
        in_specs=mk(fwd) + mk(bwd) + [
            full((4, D_MODEL)), full((1, D_MODEL)),
            full((2, LRU_BLOCKS, LRU_BLOCK_DIM, LRU_BLOCK_DIM)), full((2, D_MODEL)),
            full((2, LRU_BLOCKS, LRU_BLOCK_DIM, LRU_BLOCK_DIM)), full((2, D_MODEL)),
            full((2, D_MODEL)),
        ],
        out_specs=[
            pl.BlockSpec((ts, D_MODEL), lambda b, i: (fwd(b, i), 0)),
            pl.BlockSpec((ts, D_MODEL), lambda b, i: (bwd(b, i), 0)),
        ],
        out_shape=[jax.ShapeDtypeStruct((T, D_MODEL), F32)] * 2,
        scratch_shapes=[
            pltpu.VMEM((ts + 2 * HALO, D_MODEL), F32),
            pltpu.VMEM((ts, D_MODEL), F32),
            pltpu.VMEM((ts, D_MODEL), F32),
            pltpu.VMEM((2, D_MODEL), F32),
        ],
        compiler_params=_cparams(("parallel", "arbitrary")),
        name="rglru",
    )(proj, proj, proj, proj, proj, proj, cw, cb, wr, br, wi, bi, lam)


def _merge_body(x_ref, hmf_ref, hmb_ref, opre_ref, hlf_ref, hlb_ref, lgate_ref, gm_ref, gl_ref,
                ng_ref, wbm_ref, wbl_ref, wout_ref, g2_ref, x1_ref, h2t_ref):
    hm = hmf_ref[...] + hmb_ref[...]
    parts = []
    for h in range(N_HEADS):
        sl = slice(h * HEAD_DIM, (h + 1) * HEAD_DIM)
        hh = hm[:, sl]
        ms = jnp.mean(hh * hh, axis=-1, keepdims=True)
        parts.append(hh * lax.rsqrt(ms + EPS) * ng_ref[:, sl])
    hn = jnp.concatenate(parts, axis=-1)
    y_m = (_sigmoid(opre_ref[...]) * hn).astype(BF16)
    y_l = ((hlf_ref[...] + hlb_ref[...]) * _gelu(lgate_ref[...])).astype(BF16)
    merged = (_sigmoid(gm_ref[...]) * jnp.dot(y_m, wbm_ref[...], preferred_element_type=F32)
              + _sigmoid(gl_ref[...]) * jnp.dot(y_l, wbl_ref[...], preferred_element_type=F32))
    x1 = x_ref[...] + jnp.dot(merged.astype(BF16), wout_ref[...], preferred_element_type=F32)
    x1_ref[...] = x1
    ms = jnp.mean(x1 * x1, axis=-1, keepdims=True)
    h2 = x1 * lax.rsqrt(ms + EPS) * g2_ref[...]
    h2t_ref[...] = h2.T.astype(BF16)


def _merge(x2d, hmf, hmb, proj, hlf, hlb, ng, wbm, wbl, wout, g2, *, tm):
    T = x2d.shape[0]
    tile = lambda col: pl.BlockSpec((tm, D_MODEL), lambda i: (i, col))
    vec = pl.BlockSpec((1, D_MODEL), lambda i: (0, 0))
    mat = pl.BlockSpec((D_MODEL, D_MODEL), lambda i: (0, 0))
    return pl.pallas_call(
        _merge_body,
        grid=(T // tm,),
        in_specs=[tile(0), tile(0), tile(0), tile(1), tile(0), tile(0), tile(3), tile(4), tile(5),
                  vec, mat, mat, mat, vec],
        out_specs=[
            pl.BlockSpec((tm, D_MODEL), lambda i: (i, 0)),
            pl.BlockSpec((D_MODEL, tm), lambda i: (0, i)),
        ],
        out_shape=[
            jax.ShapeDtypeStruct((T, D_MODEL), F32),
            jax.ShapeDtypeStruct((D_MODEL, T), BF16),
        ],
        compiler_params=_cparams(("parallel",)),
        name="merge",
    )(x2d, hmf, hmb, proj, hlf, hlb, proj, proj, proj, ng, wbm, wbl, wout, g2)


def _top16_ranked(s):
    n, tb = s.shape
    iota = lax.broadcasted_iota(jnp.int32, (n, tb), 0).astype(F32)
    rank = jnp.full((n, tb), float(PEER_TOPK), F32)
    vals = []
    for j in range(PEER_TOPK):
        m = jnp.max(s, axis=0, keepdims=True)
        idx = jnp.min(jnp.where(s == m, iota, float(n)), axis=0, keepdims=True)
        sel = iota == idx
        rank = jnp.where(sel, float(j), rank)
        s = jnp.where(sel, NEG_INF, s)
        vals.append(m)
    return vals, rank


CODE_BASE = 2.0 ** 100
CODE_STEP = 2.0 ** 96
CODE_TEST = -(2.0 ** 99)


def _extract16_coded(x):
    vals = []
    for j in range(PEER_TOPK):
        m = jnp.max(x, axis=0, keepdims=True)
        x = jnp.where(x == m, -(CODE_BASE + j * CODE_STEP), x)
        vals.append(m)
    return vals, x


def _top16_ranked_notie(s):
    vals, coded = _extract16_coded(s)
    taken = coded < CODE_TEST
    rank = jnp.where(taken, (-coded - CODE_BASE) * (1.0 / CODE_STEP), float(PEER_TOPK))
    count = jnp.sum(jnp.where(taken, 1.0, 0.0), axis=0, keepdims=True)
    return vals, rank, count


_CAND_ROW_GROUPS = [(0, 0, 8), (0, 8, 8), (1, 0, 8), (2, 0, 5), (3, 0, 4), (4, 0, 3),
                    (5, 0, 2), (6, 0, 2), (7, 0, 2)]


def _pair_candidates(v1, v2):
    tb = v1[0].shape[1]
    v1_hi = jnp.concatenate(v1[8:], axis=0)
    v2_lo = jnp.concatenate(v2[:8], axis=0)
    v2_hi = jnp.concatenate(v2[8:], axis=0)
    sub = lax.broadcasted_iota(jnp.int32, (SUBLANES, tb), 0)
    cands, poss, valids = [], [], []
    for j1, base, nv in _CAND_ROW_GROUPS:
        c = v1[j1] + (v2_lo if base == 0 else v2_hi)
        cands.append(jnp.where(sub < nv, c, NEG_INF))
        poss.append((sub + (j1 * PEER_TOPK + base)).astype(F32))
        valids.append(jnp.where(sub < nv, 1.0, 0.0))
    cands.append(v1_hi + v2[0])
    poss.append(((sub + SUBLANES) * PEER_TOPK).astype(F32))
    valids.append(jnp.ones((SUBLANES, tb), F32))
    return (jnp.concatenate(cands, axis=0), jnp.concatenate(poss, axis=0),
            jnp.concatenate(valids, axis=0))


def _staircase(taken, cand, cmax):
    Z = jnp.sum(taken * jnp.exp(cand - cmax), axis=0, keepdims=True)
    grp = lambda g: taken[g * SUBLANES:(g + 1) * SUBLANES, :]
    L = [jnp.sum(grp(0) + grp(1), axis=0, keepdims=True)]
    for g in range(2, 9):
        L.append(jnp.sum(grp(g), axis=0, keepdims=True))
    last = grp(9)
    for k in range(SUBLANES):
        L.append(last[k:k + 1, :])
    return L, Z


def _select_pairs(v1, v2):
    cand0, pos, _ = _pair_candidates(v1, v2)
    cand = cand0
    taken = jnp.zeros_like(cand)
    big = float(PEER_TOPK * PEER_TOPK)
    for _ in range(PEER_TOPK):
        m = jnp.max(cand, axis=0, keepdims=True)
        p = jnp.min(jnp.where(cand == m, pos, big), axis=0, keepdims=True)
        sel = pos == p
        taken = jnp.where(sel, 1.0, taken)
        cand = jnp.where(sel, NEG_INF, cand)
    return _staircase(taken, cand0, v1[0] + v2[0])


def _select_pairs_notie(v1, v2):
    cand0, _, valid = _pair_candidates(v1, v2)
    _, coded = _extract16_coded(cand0)
    taken = jnp.where(coded < CODE_TEST, valid, 0.0)
    L, Z = _staircase(taken, cand0, v1[0] + v2[0])
    return L, Z, jnp.sum(taken, axis=0, keepdims=True)


def _store_selection(h, s1, s2, v1, rank1, v2, rank2, L, Z, r2_scr, e2_scr, lim_scr, coef_scr):
    lim = jnp.full_like(rank1, -0.5)
    for j1 in reversed(range(PEER_TOPK)):
        lim = jnp.where(rank1 < j1 + 0.5, L[j1] - 0.5, lim)
    r2_scr[h] = rank2.astype(BF16)
    e2_scr[h] = jnp.exp(s2 - v2[0]).astype(BF16)
    lim_scr[h] = lim
    coef_scr[h] = jnp.exp(s1 - v1[0]) / Z


MXU_COLS = 256


def _peer_body(h2t_ref, x1_ref, wqt_ref, keys_ref, u0_ref, u_ref, vt_ref, vtl_ref, gf_ref, y_ref,
               r2_scr, e2_scr, lim_scr, coef_scr, limb_scr, coefb_scr, act_scr, wa_scr, acc_scr,
               *, rows):
    j = pl.program_id(1)
    last = pl.num_programs(1) - 1
    tb = h2t_ref.shape[1]
    cur = j % 2
    oth = 1 - cur
    halves = [slice(c * MXU_COLS, (c + 1) * MXU_COLS) for c in range(tb // MXU_COLS)]

    def act_half(u_blk, cols):
        a = jnp.dot(u_blk[...], h2t_ref[:, cols], preferred_element_type=F32)
        return _gelu(a).astype(BF16)

    @pl.when(j == 0)
    def _():
        def select_head(h, carry):
            q_rows = pl.ds(pl.multiple_of(h * (2 * PEER_HALF), 2 * PEER_HALF), 2 * PEER_HALF)
            qt = jnp.dot(wqt_ref[q_rows, :], h2t_ref[...],
                         preferred_element_type=F32).astype(BF16)
            s1 = jnp.dot(keys_ref[2 * h], qt[:PEER_HALF], preferred_element_type=F32)
            s2 = jnp.dot(keys_ref[2 * h + 1], qt[PEER_HALF:], preferred_element_type=F32)
            out = (r2_scr, e2_scr, lim_scr, coef_scr)
            v1, rank1, n1 = _top16_ranked_notie(s1)
            v2, rank2, n2 = _top16_ranked_notie(s2)
            L, Z, n3 = _select_pairs_notie(v1, v2)
            _store_selection(h, s1, s2, v1, rank1, v2, rank2, L, Z, *out)
            k = float(PEER_TOPK)
            miscount = jnp.max(jnp.abs(n1 - k) + jnp.abs(n2 - k) + jnp.abs(n3 - k))

            @pl.when(miscount > 0.0)
            def _():
                v1, rank1 = _top16_ranked(s1)
                v2, rank2 = _top16_ranked(s2)
                L, Z = _select_pairs(v1, v2)
                _store_selection(h, s1, s2, v1, rank1, v2, rank2, L, Z, *out)
            return carry

        lax.fori_loop(0, PEER_HEADS, select_head, 0)
        acc_scr[...] = jnp.zeros_like(acc_scr)
        wa_scr[1] = jnp.zeros(wa_scr.shape[1:], BF16)
        for cols in halves:
            act_scr[0, :, cols] = act_half(u0_ref, cols)

    for h in range(PEER_HEADS):
        for ii in range(rows):
            i1 = j * rows + ii
            lim = jnp.broadcast_to(lim_scr[h, pl.ds(i1, 1), :], (BF16_ROWS, tb))
            coef = jnp.broadcast_to(coef_scr[h, pl.ds(i1, 1), :], (BF16_ROWS, tb))
            limb_scr[h * rows + ii] = lim.astype(BF16)
            coefb_scr[h * rows + ii] = coef.astype(BF16)
    def next_act(c):
        act_scr[oth, :, halves[c]] = act_half(u_ref, halves[c])

    def prev_out(c):
        acc_scr[:, halves[c]] += jnp.dot(vt_ref[...], wa_scr[oth, :, halves[c]],
                                         preferred_element_type=F32)

    n_lt = tb // LANES
    mxu_pieces = ([functools.partial(next_act, c) for c in range(len(halves))]
                  + [functools.partial(prev_out, c) for c in range(len(halves))])
    for lt in range(n_lt):
        ls = slice(lt * LANES, (lt + 1) * LANES)
        for sb in range(PEER_KEYS // BF16_ROWS):
            ss = slice(sb * BF16_ROWS, (sb + 1) * BF16_ROWS)
            accs = [None] * rows
            for h in range(PEER_HEADS):
                r2 = r2_scr[h, ss, ls]
                e2 = e2_scr[h, ss, ls]
                for ii in range(rows):
                    term = jnp.where(r2 < limb_scr[h * rows + ii, :, ls],
                                     e2 * coefb_scr[h * rows + ii, :, ls], 0.0)
                    accs[ii] = term if h == 0 else accs[ii] + term
            for ii in range(rows):
                rs = slice(ii * PEER_KEYS + sb * BF16_ROWS, ii * PEER_KEYS + (sb + 1) * BF16_ROWS)
                wa_scr[cur, rs, ls] = accs[ii] * act_scr[cur, rs, ls]
        for piece in mxu_pieces[lt * len(mxu_pieces) // n_lt:(lt + 1) * len(mxu_pieces) // n_lt]:
            piece()

    @pl.when(j == last)
    def _():
        acc = acc_scr[...] + jnp.dot(vtl_ref[...], wa_scr[cur], preferred_element_type=F32)
        x2 = x1_ref[...] + acc.T
        ms = jnp.mean(x2 * x2, axis=-1, keepdims=True)
        y_ref[...] = x2 * lax.rsqrt(ms + EPS) * gf_ref[...]


def _peer(h2t, x1, wqt, keys, u, vt, gf, *, tb, ec):
    T = x1.shape[0]
    n_chunks = u.shape[0] // ec
    n_q = wqt.shape[0]
    sel = lambda dt: pltpu.VMEM((PEER_HEADS, PEER_KEYS, tb), dt)
    rowb = lambda: pltpu.VMEM((PEER_HEADS * (ec // PEER_KEYS), BF16_ROWS, tb), BF16)
    once = dict(pipeline_mode=pl.Buffered(1))
    return pl.pallas_call(
        functools.partial(_peer_body, rows=ec // PEER_KEYS),
        grid=(T // tb, n_chunks),
        in_specs=[
            pl.BlockSpec((D_MODEL, tb), lambda i, j: (0, i)),
            pl.BlockSpec((tb, D_MODEL), lambda i, j: (i, 0)),
            pl.BlockSpec((n_q, D_MODEL), lambda i, j: (0, 0), **once),
            pl.BlockSpec((2 * PEER_HEADS, PEER_KEYS, PEER_HALF), lambda i, j: (0, 0, 0), **once),
            pl.BlockSpec((ec, D_MODEL), lambda i, j: (0, 0), **once),
            pl.BlockSpec((ec, D_MODEL), lambda i, j: (jnp.minimum(j + 1, n_chunks - 1), 0)),
            pl.BlockSpec((D_MODEL, ec), lambda i, j: (0, jnp.maximum(j - 1, 0))),
            pl.BlockSpec((D_MODEL, ec), lambda i, j: (0, n_chunks - 1), **once),
            pl.BlockSpec((1, D_MODEL), lambda i, j: (0, 0), **once),
        ],
        out_specs=pl.BlockSpec((tb, D_MODEL), lambda i, j: (i, 0)),
        out_shape=jax.ShapeDtypeStruct((T, D_MODEL), F32),
        scratch_shapes=[sel(BF16), sel(BF16), sel(F32), sel(F32), rowb(), rowb(),
                        pltpu.VMEM((2, ec, tb), BF16),
                        pltpu.VMEM((2, ec, tb), BF16),
                        pltpu.VMEM((D_MODEL, tb), F32)],
        compiler_params=_cparams(("parallel", "arbitrary")),
        name="peer",
    )(h2t, x1, wqt, keys, u, u, vt, vt, gf)


def _layer(x, p, *, tm_in, tq, L, ts, tm_merge, tb, ec):
    B, S, D = x.shape
    T = B * S
    x2d = x.reshape(T, D)
    w_in = p["w_in"]
    g0 = 2 * D_MODEL
    w_main = jnp.concatenate([w_in[:, :g0], w_in[:, g0 + N_GATES:]], axis=1).astype(BF16)
    w_gate = jnp.pad(w_in[:, g0:g0 + N_GATES], ((0, 0), (0, LANES - N_GATES))).astype(BF16)
    b_gate = jnp.pad(p["b_gates"], (0, LANES - N_GATES)).reshape(1, LANES)
    row = lambda a: a.reshape(1, -1)

    proj, gates = _in_proj(x2d, row(p["norm1_g"]), w_main, w_gate, b_gate, tm=tm_in)
    q, kt, v = _qkv(proj, p["mlstm_conv_w"], row(p["mlstm_conv_b"]),
                    p["mlstm_w_q"].astype(BF16), p["mlstm_w_k"].astype(BF16),
                    p["mlstm_w_v"].astype(BF16), S=S, tq=tq)
    hmf, hmb = _mlstm_scan(q, kt, v, gates, B=B, S=S, L=L)
    hlf, hlb = _lru(proj, p["lru_conv_w"], row(p["lru_conv_b"]),
                    p["lru_w_r"].astype(BF16), p["lru_b_r"], p["lru_w_i"].astype(BF16),
                    p["lru_b_i"], p["lru_lambda"], B=B, S=S, ts=ts)
    x1, h2t = _merge(x2d, hmf, hmb, proj, hlf, hlb, row(p["mlstm_norm_g"]),
                     p["w_branch_mlstm"].astype(BF16), p["w_branch_lru"].astype(BF16),
                     p["w_out"].astype(BF16), row(p["norm2_g"]), tm=tm_merge)
    n_keys = PEER_HEADS * 2
    keys = p["peer_sub_keys"].reshape(n_keys, PEER_KEYS, PEER_HALF).astype(BF16)
    y = _peer(h2t, x1, p["peer_w_q"].T.astype(BF16), keys, p["peer_u"].astype(BF16),
              p["peer_v"].T.astype(BF16), row(p["final_norm_g"]), tb=tb, ec=ec)
    return y.reshape(B, S, D)


def kernel(x, norm1_g, w_in, b_gates, mlstm_conv_w, mlstm_conv_b, mlstm_w_q, mlstm_w_k, mlstm_w_v,
           mlstm_norm_g, lru_conv_w, lru_conv_b, lru_w_r, lru_b_r, lru_w_i, lru_b_i, lru_lambda,
           w_branch_mlstm, w_branch_lru, w_out, norm2_g, peer_w_q, peer_sub_keys, peer_u, peer_v,
           final_norm_g):
    p = dict(norm1_g=norm1_g[0], w_in=w_in[0], b_gates=b_gates[0], mlstm_conv_w=mlstm_conv_w[0],
             mlstm_conv_b=mlstm_conv_b[0], mlstm_w_q=mlstm_w_q[0], mlstm_w_k=mlstm_w_k[0],
             mlstm_w_v=mlstm_w_v[0], mlstm_norm_g=mlstm_norm_g[0], lru_conv_w=lru_conv_w[0],
             lru_conv_b=lru_conv_b[0], lru_w_r=lru_w_r[0], lru_b_r=lru_b_r[0], lru_w_i=lru_w_i[0],
             lru_b_i=lru_b_i[0], lru_lambda=lru_lambda[0], w_branch_mlstm=w_branch_mlstm[0],
             w_branch_lru=w_branch_lru[0], w_out=w_out[0], norm2_g=norm2_g[0],
             peer_w_q=peer_w_q[0], peer_sub_keys=peer_sub_keys[0], peer_u=peer_u[0],
             peer_v=peer_v[0], final_norm_g=final_norm_g)
    return _layer(x, p, tm_in=2048, tq=1024, L=256, ts=1024, tm_merge=256, tb=512, ec=1024)
```
